```python
import math
import jax, jax.numpy as jnp
from jax import lax
import numpy as np

D_MODEL = 2048
BATCH = 4
SEQ = 4096
DEPTH = 2

MIX_WIDTH = D_MODEL
DIFF_WIDTH = MIX_WIDTH // 2
RWKV_WIDTH = MIX_WIDTH - DIFF_WIDTH
DIFF_QK_DIM = 64
DIFF_V_DIM = 2 * DIFF_QK_DIM
DIFF_HEADS = DIFF_WIDTH // DIFF_V_DIM
ROPE_DIM = DIFF_QK_DIM // 4
ROPE_THETA = 500000.0
Q_BLOCK = 128
DIFF_NORM_EPS = 1e-5
RWKV_HEAD = 64
RWKV_HEADS = RWKV_WIDTH // RWKV_HEAD
DECAY_LORA = 64
ICLR_LORA = 64
GATE_LORA = 128
DECAY_SCALE = 0.606531
GN_EPS = 64e-5
L2_EPS = 1e-12
N_EXPERTS = 16
N_GROUPS = 4
EXPERTS_PER_GROUP = N_EXPERTS // N_GROUPS
TOP_K = 2
D_FF_EXPERT = D_MODEL // 2
MOE_BLOCK = 128
RMS_EPS = 1e-6
N_MOD = 6
DIFF_COLS = 3 * DIFF_WIDTH
RWKV_COLS = 3 * RWKV_WIDTH + 2 * DECAY_LORA + 2 * ICLR_LORA + GATE_LORA
IN_COLS = DIFF_COLS + RWKV_COLS

kernel_name = "hymba_diffattn_rwkv7_grouped_moe_adaln_encoder"


def rms_norm(x, gain, eps=RMS_EPS):
    xf = x.astype(jnp.float32)
    y = xf * lax.rsqrt(jnp.mean(xf * xf, axis=-1, keepdims=True) + eps)
    return (y * gain.astype(jnp.float32)).astype(x.dtype)


def rope_partial(t, cos, sin):
    half = ROPE_DIM // 2
    r1, r2, rest = t[..., :half], t[..., half:ROPE_DIM], t[..., ROPE_DIM:]
    return jnp.concatenate([r1 * cos - r2 * sin, r2 * cos + r1 * sin, rest], axis=-1)


def diff_attention(q, k, v, lam, lam_init, sub_gain):
    B, S = q.shape[0], q.shape[1]
    nblk = S // Q_BLOCK
    scale = DIFF_QK_DIM ** -0.5
    qb = jnp.moveaxis(q.reshape(B, nblk, Q_BLOCK, DIFF_HEADS, 2, DIFF_QK_DIM), 1, 0)
    kf = k.astype(jnp.float32)
    vf = v.astype(jnp.float32)

    def block(q_blk):
        s = jnp.einsum('bqhcd,bkhcd->bhcqk', q_blk.astype(jnp.float32), kf) * scale
        p = jax.nn.softmax(s, axis=-1)
        w = p[:, :, 0] - lam * p[:, :, 1]
        return jnp.einsum('bhqk,bkhe->bqhe', w, vf)

    o = lax.map(block, qb)
    o = jnp.moveaxis(o, 0, 1).reshape(B, S, DIFF_HEADS, DIFF_V_DIM)
    o = rms_norm(o, sub_gain, DIFF_NORM_EPS) * (1.0 - lam_init)
    return o.reshape(B, S, DIFF_WIDTH).astype(q.dtype)


def centred_shift(p, mu_prev, mu_next):
    prev = jnp.pad(p[:, :-1], ((0, 0), (1, 0), (0, 0)))
    nxt = jnp.pad(p[:, 1:], ((0, 0), (0, 1), (0, 0)))
    return p + mu_prev * (prev - p) + mu_next * (nxt - p)


def rwkv7_step(state, inp):
    r, w, k, v, kk, b = inp
    sa = jnp.einsum('dbhij,dbhj->dbhi', state, -kk)
    state = state * w[..., None, :] + sa[..., :, None] * b[..., None, :] + v[..., :, None] * k[..., None, :]
    y = jnp.einsum('dbhij,dbhj->dbhi', state, r)
    return state, y


def rwkv7_mixer(p, mu_prev, mu_next, w0, w_up, a0, a_up, k_k, k_a, r_k, g_up, gn_gain, gn_bias):
    B, S = p.shape[0], p.shape[1]
    W = RWKV_WIDTH
    dt = p.dtype
    p = centred_shift(p, mu_prev, mu_next).astype(jnp.float32)
    r, k, v = p[..., :W], p[..., W:2 * W], p[..., 2 * W:3 * W]
    o = 3 * W
    wd = p[..., o:o + 2 * DECAY_LORA].reshape(B, S, 2, DECAY_LORA)
    o += 2 * DECAY_LORA
    ad = p[..., o:o + 2 * ICLR_LORA].reshape(B, S, 2, ICLR_LORA)
    o += 2 * ICLR_LORA
    gd = p[..., o:o + GATE_LORA]
    decay = jnp.exp(-DECAY_SCALE * jax.nn.sigmoid(w0 + jnp.einsum('bsdr,drc->bsdc', jnp.tanh(wd), w_up)))
    a = jax.nn.sigmoid(a0 + jnp.einsum('bsdr,drc->bsdc', ad, a_up))
    g = jnp.einsum('bsr,rc->bsc', jax.nn.sigmoid(gd), g_up)
    hd = lambda t: t.reshape(t.shape[:-1] + (RWKV_HEADS, RWKV_HEAD))
    kk = hd(k * k_k)
    kk = kk / jnp.maximum(jnp.sqrt(jnp.sum(kk * kk, axis=-1, keepdims=True)), L2_EPS)
    kk = kk.reshape(B, S, W)
    k_dir = k[:, :, None, :] * (1.0 + (a - 1.0) * k_a)

    def dirs(t):
        t = jnp.stack([t[:, :, 0], jnp.flip(t[:, :, 1], axis=1)], axis=0)
        return jnp.moveaxis(t.reshape(2, B, S, RWKV_HEADS, RWKV_HEAD), 2, 0)

    both = lambda t: jnp.stack([t, t], axis=2)
    xs = (dirs(both(r)), dirs(decay), dirs(k_dir), dirs(both(v)), dirs(both(kk)), dirs(kk[:, :, None, :] * a))
    state0 = jnp.zeros((2, B, RWKV_HEADS, RWKV_HEAD, RWKV_HEAD), jnp.float32)
    _, y = lax.scan(rwkv7_step, state0, xs)
    y = jnp.moveaxis(y, 0, 2)
    y = y[0] + jnp.flip(y[1], axis=1)
    mean = jnp.mean(y, axis=-1, keepdims=True)
    var = jnp.mean(jnp.square(y - mean), axis=-1, keepdims=True)
    yn = ((y - mean) * lax.rsqrt(var + GN_EPS)).reshape(B, S, W) * gn_gain + gn_bias
    bonus = jnp.sum(hd(r) * hd(k_dir[:, :, 0] + k_dir[:, :, 1]) * r_k.reshape(RWKV_HEADS, RWKV_HEAD),
                    axis=-1, keepdims=True) * hd(v)
    return ((yn + bonus.reshape(B, S, W)) * g).astype(dt)


def moe(h, w_router, router_bias, w_gate, w_up, w_down):
    B, S, D = h.shape
    T = B * S
    hf = h.reshape(T, D)
    scores = jax.nn.sigmoid((hf @ w_router).astype(jnp.float32))
    sel = (scores + router_bias.astype(jnp.float32)).reshape(T, N_GROUPS, EXPERTS_PER_GROUP)
    group_score = jnp.sum(lax.top_k(sel, TOP_K)[0], axis=-1)
    grp = jnp.argmax(group_score, axis=-1).astype(jnp.int32)
    in_grp = jnp.take_along_axis(sel, grp[:, None, None], axis=1)[:, 0]
    _, local = lax.top_k(in_grp, TOP_K)
    expert = grp[:, None] * EXPERTS_PER_GROUP + local.astype(jnp.int32)
    gate = jnp.take_along_axis(scores, expert, axis=1)
    gate = gate / jnp.sum(gate, axis=-1, keepdims=True)
    A = T * TOP_K
    flat_e = expert.reshape(A)
    flat_tok = jnp.repeat(jnp.arange(T, dtype=jnp.int32), TOP_K)
    flat_w = gate.reshape(A)
    order = jnp.argsort(flat_e)
    sorted_e = flat_e[order]
    counts = jnp.bincount(flat_e, length=N_EXPERTS).astype(jnp.int32)
    offsets = jnp.cumsum(counts) - counts
    padded = ((counts + MOE_BLOCK - 1) // MOE_BLOCK) * MOE_BLOCK
    padded_ends = jnp.cumsum(padded)
    padded_offsets = padded_ends - padded
    dest = padded_offsets[sorted_e] + jnp.arange(A, dtype=jnp.int32) - offsets[sorted_e]
    P = A + N_EXPERTS * MOE_BLOCK
    nb = P // MOE_BLOCK
    row_tok = jnp.full((P,), T, jnp.int32).at[dest].set(flat_tok[order])
    row_w = jnp.zeros((P,), jnp.float32).at[dest].set(flat_w[order])
    block_start = jnp.arange(nb, dtype=jnp.int32) * MOE_BLOCK
    block_e = jnp.minimum(jnp.searchsorted(padded_ends, block_start, side='right'), N_EXPERTS - 1).astype(jnp.int32)
    h_pad = jnp.concatenate([hf, jnp.zeros((1, D), hf.dtype)], axis=0)
    xr = h_pad[row_tok].reshape(nb, MOE_BLOCK, D)

    def expert_block(args):
        xb, e = args
        return (jax.nn.silu(xb @ w_gate[e]) * (xb @ w_up[e])) @ w_down[e]

    yr = lax.map(expert_block, (xr, block_e)).reshape(P, D)
    y = jax.ops.segment_sum(yr * row_w[:, None], row_tok, num_segments=T + 1)[:T]
    return y.reshape(B, S, D).astype(h.dtype)


def setup_inputs(seed: int = 0) -> dict:
    key = jax.random.key(seed)
    ks = iter(jax.random.split(key, 40))
    nrm = lambda shape, s: jax.random.normal(next(ks), shape, jnp.float32) * s
    W = RWKV_WIDTH
    x = nrm((BATCH, SEQ, D_MODEL), 1.0)
    c = nrm((BATCH, D_MODEL), 1.0)
    positions = (jnp.arange(SEQ, dtype=jnp.int32)[None, :]
                 + jax.random.randint(next(ks), (BATCH, 1), 0, 1024, dtype=jnp.int32))
    return {
        "x": x,
        "c": c,
        "positions": positions,
        "w_ada": nrm((DEPTH, D_MODEL, N_MOD * D_MODEL), 0.5 * D_MODEL ** -0.5),
        "b_ada": nrm((DEPTH, N_MOD * D_MODEL), 0.02),
        "norm_gain": 1.0 + nrm((DEPTH, 2, D_MODEL), 0.02),
        "w_in": nrm((DEPTH, D_MODEL, IN_COLS), D_MODEL ** -0.5),
        "w_out": nrm((DEPTH, MIX_WIDTH, D_MODEL), MIX_WIDTH ** -0.5),
        "mu_prev": jax.random.uniform(next(ks), (DEPTH, RWKV_COLS), jnp.float32, 0.0, 0.5),
        "mu_next": jax.random.uniform(next(ks), (DEPTH, RWKV_COLS), jnp.float32, 0.0, 0.5),
        "decay_w0": nrm((DEPTH, 2, W), 1.0),
        "decay_up": nrm((DEPTH, 2, DECAY_LORA, W), 0.5 * DECAY_LORA ** -0.5),
        "iclr_a0": nrm((DEPTH, 2, W), 0.5),
        "iclr_up": nrm((DEPTH, 2, ICLR_LORA, W), 0.5 * ICLR_LORA ** -0.5),
        "k_k": 0.85 + nrm((DEPTH, W), 0.05),
        "k_a": 1.0 + nrm((DEPTH, W), 0.05),
        "r_k": nrm((DEPTH, W), 0.1),
        "gate_up": nrm((DEPTH, GATE_LORA, W), GATE_LORA ** -0.5),
        "gn_gain": 1.0 + nrm((DEPTH, W), 0.02),
        "gn_bias": nrm((DEPTH, W), 0.02),
        "lam_q1": nrm((DEPTH, DIFF_QK_DIM), 0.1),
        "lam_k1": nrm((DEPTH, DIFF_QK_DIM), 0.1),
        "lam_q2": nrm((DEPTH, DIFF_QK_DIM), 0.1),
        "lam_k2": nrm((DEPTH, DIFF_QK_DIM), 0.1),
        "diff_sub_gain": 1.0 + nrm((DEPTH, DIFF_V_DIM), 0.02),
        "w_router": nrm((D_MODEL, N_EXPERTS), D_MODEL ** -0.5),
        "router_bias": nrm((N_EXPERTS,), 0.01),
        "w_gate": nrm((DEPTH, N_EXPERTS, D_MODEL, D_FF_EXPERT), D_MODEL ** -0.5),
        "w_up": nrm((DEPTH, N_EXPERTS, D_MODEL, D_FF_EXPERT), D_MODEL ** -0.5),
        "w_down": nrm((DEPTH, N_EXPERTS, D_FF_EXPERT, D_MODEL), D_FF_EXPERT ** -0.5),
        "final_gain": 1.0 + nrm((D_MODEL,), 0.02),
    }


def reference(x, c, positions, w_ada, b_ada, norm_gain, w_in, w_out, mu_prev, mu_next,
              decay_w0, decay_up, iclr_a0, iclr_up, k_k, k_a, r_k, gate_up, gn_gain, gn_bias,
              lam_q1, lam_k1, lam_q2, lam_k2, diff_sub_gain, w_router, router_bias,
              w_gate, w_up, w_down, final_gain):
    B, S, D = x.shape
    inv_freq = 1.0 / (ROPE_THETA ** (jnp.arange(0, ROPE_DIM, 2, dtype=jnp.float32) / ROPE_DIM))
    ang = positions.astype(jnp.float32)[..., None] * inv_freq
    cos = jnp.cos(ang)[:, :, None, None, :].astype(x.dtype)
    sin = jnp.sin(ang)[:, :, None, None, :].astype(x.dtype)
    c_act = jax.nn.silu(c)
    for l in range(DEPTH):
        mod = (c_act @ w_ada[l] + b_ada[l])[:, None, :]
        sh1, sc1, g1, sh2, sc2, g2 = jnp.split(mod, N_MOD, axis=-1)
        h = rms_norm(x, norm_gain[l, 0]) * (1.0 + sc1) + sh1
        proj = h @ w_in[l]
        q = proj[..., :DIFF_WIDTH].reshape(B, S, DIFF_HEADS, 2, DIFF_QK_DIM)
        k = proj[..., DIFF_WIDTH:2 * DIFF_WIDTH].reshape(B, S, DIFF_HEADS, 2, DIFF_QK_DIM)
        v = proj[..., 2 * DIFF_WIDTH:DIFF_COLS].reshape(B, S, DIFF_HEADS, DIFF_V_DIM)
        q = rope_partial(q, cos, sin)
        k = rope_partial(k, cos, sin)
        lam_init = 0.8 - 0.6 * math.exp(-0.3 * l)
        lam = (jnp.exp(jnp.sum(lam_q1[l].astype(jnp.float32) * lam_k1[l].astype(jnp.float32)))
               - jnp.exp(jnp.sum(lam_q2[l].astype(jnp.float32) * lam_k2[l].astype(jnp.float32)))
               + lam_init)
        y_diff = diff_attention(q, k, v, lam, lam_init, diff_sub_gain[l])
        y_rwkv = rwkv7_mixer(proj[..., DIFF_COLS:], mu_prev[l], mu_next[l], decay_w0[l], decay_up[l],
                             iclr_a0[l], iclr_up[l], k_k[l], k_a[l], r_k[l], gate_up[l],
                             gn_gain[l], gn_bias[l])
        mix = jnp.concatenate([y_diff, y_rwkv], axis=-1) @ w_out[l]
        x = x + g1 * mix
        h2 = rms_norm(x, norm_gain[l, 1]) * (1.0 + sc2) + sh2
        x = x + g2 * moe(h2, w_router, router_bias, w_gate[l], w_up[l], w_down[l])
    return rms_norm(x, final_gain)
```

```python
import functools
import math

import jax
import jax.numpy as jnp
from jax import lax
from jax.experimental import pallas as pl
from jax.experimental.pallas import tpu as pltpu

F32 = jnp.float32
BF16 = jnp.bfloat16
U32 = jnp.uint32
HIGHEST = lax.Precision.HIGHEST

D_MODEL = 2048
DIFF_WIDTH = 1024
RWKV_WIDTH = 1024
DIFF_QK_DIM = 64
DIFF_V_DIM = 128
DIFF_HEADS = 8
ROPE_DIM = 16
ROPE_THETA = 500000.0
DIFF_NORM_EPS = 1e-5
RWKV_HEAD = 64
DECAY_LORA = 64
ICLR_LORA = 64
GATE_LORA = 128
DECAY_SCALE = 0.606531
GN_EPS = 64e-5
L2_EPS = 1e-12
N_EXPERTS = 16
N_GROUPS = 4
EXPERTS_PER_GROUP = 4
TOP_K = 2
D_FF = 1024
RMS_EPS = 1e-6
N_MOD = 6
DIFF_COLS = 3 * DIFF_WIDTH
RWKV_COLS = 3 * RWKV_WIDTH + 2 * DECAY_LORA + 2 * ICLR_LORA + GATE_LORA
LORA_COLS = RWKV_COLS - 3 * RWKV_WIDTH

LANES = 128
SCAN_CHUNK = 64
SCAN_GROUP = 256
MOE_BM = 256
VMEM_LIMIT = 56 * 1024 * 1024


def _cparams(sem):
    return pltpu.CompilerParams(dimension_semantics=sem, vmem_limit_bytes=VMEM_LIMIT)


def _tile(n, pref):
    t = min(n, pref)
    assert n % t == 0, (n, t)
    return t


def _mod_kernel(c_ref, w_ref, b_ref, o_ref):
    c = c_ref[...]
    ca = (c * jax.nn.sigmoid(c)).astype(BF16)
    o_ref[0] = jnp.dot(ca, w_ref[0].astype(BF16), preferred_element_type=F32) + b_ref[0]


def _ada_mod(c, w_ada, b_ada):
    depth, d, n = w_ada.shape
    b = c.shape[0]
    cp = jnp.zeros((8, d), F32).at[:b].set(c)
    tn = 1024
    out = pl.pallas_call(
        _mod_kernel,
        grid=(depth, n // tn),
        in_specs=[pl.BlockSpec((8, d), lambda l, j: (0, 0)),
                  pl.BlockSpec((1, d, tn), lambda l, j: (l, 0, j)),
                  pl.BlockSpec((1, 1, tn), lambda l, j: (l, 0, j))],
        out_specs=pl.BlockSpec((1, 8, tn), lambda l, j: (l, 0, j)),
        out_shape=jax.ShapeDtypeStruct((depth, 8, n), F32),
        compiler_params=_cparams(("arbitrary", "arbitrary")),
        name="ada_mod",
    )(cp, w_ada, b_ada.reshape(depth, 1, n))
    return out[:, :b]


def _rope_kernel(pos_ref, invf_ref, m1_ref, m2_ref, c_ref, s1_ref, s2_ref):
    ang = pos_ref[...] * invf_ref[...]
    s = jnp.sin(ang)
    c_ref[...] = jnp.cos(ang)
    s1_ref[...] = -s * m1_ref[...]
    s2_ref[...] = s * m2_ref[...]


def _rope_tables(positions):
    t = positions.size
    half = ROPE_DIM // 2
    inv_freq = 1.0 / (ROPE_THETA ** (jnp.arange(0, ROPE_DIM, 2, dtype=F32) / ROPE_DIM))
    lane = jnp.arange(LANES) % DIFF_QK_DIM
    m1 = (lane < half).astype(F32)
    m2 = ((lane >= half) & (lane < ROPE_DIM)).astype(F32)
    invf = jnp.where(lane < ROPE_DIM, inv_freq[lane % half], 0.0).astype(F32)
    pos = jnp.broadcast_to(positions.astype(F32).reshape(t, 1), (t, LANES))
    tm = _tile(t, 512)
    row = pl.BlockSpec((1, LANES), lambda i: (0, 0))
    blk = pl.BlockSpec((tm, LANES), lambda i: (i, 0))
    return pl.pallas_call(
        _rope_kernel,
        grid=(t // tm,),
        in_specs=[blk, row, row, row],
        out_specs=[blk, blk, blk],
        out_shape=[jax.ShapeDtypeStruct((t, LANES), F32)] * 3,
        compiler_params=_cparams(("arbitrary",)),
        name="rope_tables",
    )(pos, invf.reshape(1, LANES), m1.reshape(1, LANES), m2.reshape(1, LANES))


def _norm_mod(x, gain, sc, sh):
    ms = jnp.mean(x * x, axis=-1, keepdims=True)
    return (x * lax.rsqrt(ms + RMS_EPS) * gain) * (1.0 + sc) + sh


def _proj_kernel(x_ref, gain_ref, sc_ref, sh_ref, w_ref, *rest, rope):
    if rope:
        c_ref, s1_ref, s2_ref, o_ref, h_ref = rest
    else:
        o_ref, h_ref = rest
    j = pl.program_id(1)

    @pl.when(j == 0)
    def _():
        h_ref[...] = _norm_mod(x_ref[...], gain_ref[...], sc_ref[0], sh_ref[0]).astype(BF16)

    y = jnp.dot(h_ref[...], w_ref[...], preferred_element_type=F32)
    if not rope:
        o_ref[...] = y.astype(o_ref.dtype)
        return

    @pl.when(j < 2)
    def _():
        scale = jnp.where(j == 0, DIFF_QK_DIM ** -0.5, 1.0).astype(F32)
        c = c_ref[...]
        s1 = s1_ref[...]
        s2 = s2_ref[...]
        for s in range(y.shape[1] // LANES):
            ys = y[:, s * LANES:(s + 1) * LANES]
            o = ys * c + pltpu.roll(ys, LANES - ROPE_DIM // 2, 1) * s1 + pltpu.roll(ys, ROPE_DIM // 2, 1) * s2
            o_ref[:, s * LANES:(s + 1) * LANES] = (o * scale).astype(o_ref.dtype)

    @pl.when(j == 2)
    def _():
        o_ref[...] = y.astype(o_ref.dtype)


def _project(x2, gain, sc, sh, w, seq, out_dtype, tn, rope_tabs=None):
    t, d = x2.shape
    n = w.shape[1]
    tm = _tile(seq, 512)
    tps = seq // tm
    rope = rope_tabs is not None
    in_specs = [pl.BlockSpec((tm, d), lambda i, j: (i, 0)),
                pl.BlockSpec((1, d), lambda i, j: (0, 0)),
                pl.BlockSpec((1, 1, d), lambda i, j: (i // tps, 0, 0)),
                pl.BlockSpec((1, 1, d), lambda i, j: (i // tps, 0, 0)),
                pl.BlockSpec((d, tn), lambda i, j: (0, j))]
    args = [x2, gain.reshape(1, d), sc, sh, w]
    if rope:
        in_specs += [pl.BlockSpec((tm, LANES), lambda i, j: (i, 0))] * 3
        args += list(rope_tabs)
    return pl.pallas_call(
        functools.partial(_proj_kernel, rope=rope),
        grid=(t // tm, n // tn),
        in_specs=in_specs,
        out_specs=pl.BlockSpec((tm, tn), lambda i, j: (i, j)),
        out_shape=jax.ShapeDtypeStruct((t, n), out_dtype),
        scratch_shapes=[pltpu.VMEM((tm, d), BF16)],
        compiler_params=_cparams(("arbitrary", "arbitrary")),
        name="proj_qkv" if rope else "proj_rwkv",
    )(*args)


def _attn_kernel(q_ref, k_ref, v_ref, lam_ref, gain_ref, o_ref, *, tk, out_scale):
    tq = q_ref.shape[0]
    seq = k_ref.shape[0]
    q = q_ref[...]
    lane = lax.broadcasted_iota(jnp.int32, q.shape, 1)
    zero = jnp.zeros_like(q)
    qs = jnp.concatenate([jnp.where(lane < DIFF_QK_DIM, q, zero),
                          jnp.where(lane >= DIFF_QK_DIM, q, zero)], axis=0)

    def body(c, carry):
        m, l, acc = carry
        off = pl.multiple_of(c * tk, tk)
        kc = k_ref[pl.ds(off, tk), :]
        vc = v_ref[pl.ds(off, tk), :]
        s = lax.dot_general(qs, kc, (((1,), (1,)), ((), ())), preferred_element_type=F32)
        m_new = jnp.maximum(m, jnp.max(s, axis=-1, keepdims=True))
        alpha = jnp.exp(m - m_new)
        p = jnp.exp(s - m_new)
        l = alpha * l + jnp.sum(p, axis=-1, keepdims=True)
        acc = alpha * acc + jnp.dot(p.astype(BF16), vc, preferred_element_type=F32)
        return m_new, l, acc

    init = (jnp.full((2 * tq, 1), -1e30, F32), jnp.zeros((2 * tq, 1), F32),
            jnp.zeros((2 * tq, DIFF_V_DIM), F32))
    _, l, acc = lax.fori_loop(0, seq // tk, body, init)
    o = acc / l
    d = o[:tq] - lam_ref[...] * o[tq:]
    ms = jnp.mean(d * d, axis=-1, keepdims=True)
    y = (d * lax.rsqrt(ms + DIFF_NORM_EPS) * gain_ref[...]) * out_scale
    o_ref[...] = y.astype(o_ref.dtype)


def _diff_attention(qkv, lam, sub_gain, lam_init, batch, seq):
    t = qkv.shape[0]
    tq = _tile(seq, 256)
    tk = _tile(seq, 512)
    nq = seq // tq
    h = DIFF_HEADS
    return pl.pallas_call(
        functools.partial(_attn_kernel, tk=tk, out_scale=1.0 - lam_init),
        grid=(batch, h, nq),
        in_specs=[pl.BlockSpec((tq, DIFF_V_DIM), lambda b, hh, qi: (b * nq + qi, hh)),
                  pl.BlockSpec((seq, DIFF_V_DIM), lambda b, hh, qi: (b, h + hh)),
                  pl.BlockSpec((seq, DIFF_V_DIM), lambda b, hh, qi: (b, 2 * h + hh)),
                  pl.BlockSpec((1, DIFF_V_DIM), lambda b, hh, qi: (0, 0)),
                  pl.BlockSpec((1, DIFF_V_DIM), lambda b, hh, qi: (0, 0))],
        out_specs=pl.BlockSpec((tq, DIFF_V_DIM), lambda b, hh, qi: (b * nq + qi, hh)),
        out_shape=jax.ShapeDtypeStruct((t, DIFF_WIDTH), BF16),
        compiler_params=_cparams(("arbitrary", "arbitrary", "arbitrary")),
        name="diff_attn",
    )(qkv, qkv, qkv, jnp.broadcast_to(lam.astype(F32).reshape(1, 1), (1, DIFF_V_DIM)),
      sub_gain.astype(F32).reshape(1, DIFF_V_DIM))


def _head_ones(n):
    r = lax.broadcasted_iota(jnp.int32, (n, n), 0) >> 6
    c = lax.broadcasted_iota(jnp.int32, (n, n), 1) >> 6
    return jnp.where(r == c, 1.0, 0.0).astype(BF16)


def _head_sum(x):
    ones = _head_ones(LANES)
    outs = []
    for s in range(x.shape[1] // LANES):
        xs = x[:, s * LANES:(s + 1) * LANES]
        hi = xs.astype(BF16)
        lo = (xs - hi.astype(F32)).astype(BF16)
        outs.append(jnp.dot(hi, ones, preferred_element_type=F32) + jnp.dot(lo, ones, preferred_element_type=F32))
    return jnp.concatenate(outs, axis=1)


def _rwkv_prep_kernel(p_ref, pp_ref, pn_ref, mup_ref, mun_ref, wdec_ref, w0_ref, wa_ref, a0_ref, gup_ref, kkw_ref,
                      r_o, k_o, v_o, kk_o, lw0_o, lw1_o, a0_o, a1_o, g_o, *, tiles_per_seq):
    i = pl.program_id(0)
    tm = p_ref.shape[0]
    w = RWKV_WIDTH
    first = (i % tiles_per_seq) == 0
    last = (i % tiles_per_seq) == tiles_per_seq - 1

    def shifted(c0, c1):
        p = p_ref[:, c0:c1]
        prow = jnp.where(first, 0.0, pp_ref[7:8, c0:c1])
        nrow = jnp.where(last, 0.0, pn_ref[0:1, c0:c1])
        row = lax.broadcasted_iota(jnp.int32, p.shape, 0)
        prev = jnp.where(row == 0, prow, pltpu.roll(p, 1, 0))
        nxt = jnp.where(row == tm - 1, nrow, pltpu.roll(p, tm - 1, 0))
        return p + mup_ref[:, c0:c1] * (prev - p) + mun_ref[:, c0:c1] * (nxt - p)

    lo = shifted(3 * w, 3 * w + LORA_COLS)
    wd = jnp.tanh(lo[:, :2 * DECAY_LORA]).astype(BF16)
    ad = lo[:, 2 * DECAY_LORA:2 * DECAY_LORA + 2 * ICLR_LORA].astype(BF16)
    gd = jax.nn.sigmoid(lo[:, 2 * DECAY_LORA + 2 * ICLR_LORA:]).astype(BF16)
    lw = -DECAY_SCALE * jax.nn.sigmoid(w0_ref[...] + jnp.dot(wd, wdec_ref[...], preferred_element_type=F32))
    lw0_o[...] = lw[:, :w]
    lw1_o[...] = lw[:, w:]
    a = jax.nn.sigmoid(a0_ref[...] + jnp.dot(ad, wa_ref[...], preferred_element_type=F32))
    a0_o[...] = a[:, :w]
    a1_o[...] = a[:, w:]
    g_o[...] = jnp.dot(gd, gup_ref[...], preferred_element_type=F32)
    r_o[...] = shifted(0, w)
    k = shifted(w, 2 * w)
    k_o[...] = k
    v_o[...] = shifted(2 * w, 3 * w)
    kk = k * kkw_ref[...]
    nrm = jnp.sqrt(_head_sum(kk * kk))
    kk_o[...] = kk / jnp.maximum(nrm, L2_EPS)


def _rwkv_prep(p, mu_prev, mu_next, decay_w0, decay_up, iclr_a0, iclr_up, k_k, gate_up, seq):
    t = p.shape[0]
    w = RWKV_WIDTH
    tm = _tile(seq, 256)
    tps = seq // tm
    nblk8 = t // 8
    z = jnp.zeros((DECAY_LORA, w), F32)
    wdec = jnp.concatenate([jnp.concatenate([decay_up[0], z], 0), jnp.concatenate([z, decay_up[1]], 0)], 1).astype(BF16)
    wa = jnp.concatenate([jnp.concatenate([iclr_up[0], z], 0), jnp.concatenate([z, iclr_up[1]], 0)], 1).astype(BF16)
    full = lambda shape: pl.BlockSpec(shape, lambda i: (0,) * len(shape))
    blk = pl.BlockSpec((tm, w), lambda i: (i, 0))
    outs = pl.pallas_call(
        functools.partial(_rwkv_prep_kernel, tiles_per_seq=tps),
        grid=(t // tm,),
        in_specs=[pl.BlockSpec((tm, RWKV_COLS), lambda i: (i, 0)),
                  pl.BlockSpec((8, RWKV_COLS), lambda i: (jnp.maximum(i * (tm // 8) - 1, 0), 0)),
                  pl.BlockSpec((8, RWKV_COLS), lambda i: (jnp.minimum((i + 1) * (tm // 8), nblk8 - 1), 0)),
                  full((1, RWKV_COLS)), full((1, RWKV_COLS)),
                  full((2 * DECAY_LORA, 2 * w)), full((1, 2 * w)),
                  full((2 * ICLR_LORA, 2 * w)), full((1, 2 * w)),
                  full((GATE_LORA, w)), full((1, w))],
        out_specs=[blk] * 9,
        out_shape=[jax.ShapeDtypeStruct((t, w), F32)] * 9,
        compiler_params=_cparams(("arbitrary",)),
        name="rwkv_prep",
    )(p, p, p, mu_prev.reshape(1, -1), mu_next.reshape(1, -1), wdec, decay_w0.reshape(1, 2 * w),
      wa, iclr_a0.reshape(1, 2 * w), gate_up.astype(BF16), k_k.reshape(1, w))
    return outs


def _scan_chunk(r, k, v, kk, lw, a, ka, st_ref, fwd):
    c = r.shape[0]
    g = r.shape[1]
    nh = g // RWKV_HEAD
    hp = dict(precision=HIGHEST, preferred_element_type=F32)
    lp = dict(preferred_element_type=F32)
    nt = (((1,), (1,)), ((), ()))
    tn = (((0,), (0,)), ((), ()))

    ti = lax.broadcasted_iota(jnp.int32, (c, c), 0)
    si = lax.broadcasted_iota(jnp.int32, (c, c), 1)
    tri = jnp.where((si <= ti) if fwd else (si >= ti), 1.0, 0.0).astype(F32)
    lc = jnp.dot(tri, lw, **hp)
    lex = lc - lw
    ltot = lc[c - 1:c, :] if fwd else lc[0:1, :]
    e_inc = jnp.exp(lc)
    e_inv = jnp.exp(-lc)
    e_rem = jnp.exp(ltot - lc)
    kd = k * (1.0 + (a - 1.0) * ka)
    b = kk * a
    a_t = (-kk * jnp.exp(lex)).astype(BF16)
    r_t = (r * e_inc).astype(BF16)
    b_t = (b * e_inv).astype(BF16)
    k_t = (kd * e_inv).astype(BF16)
    b_h = (b * e_rem).astype(BF16)
    k_h = (kd * e_rem).astype(BF16)
    vb = v.astype(BF16)

    lane_head = lax.broadcasted_iota(jnp.int32, (c, g), 1) >> 6

    def bd(x):
        z = jnp.zeros_like(x)
        return jnp.concatenate([jnp.where(lane_head == h, x, z) for h in range(nh)], axis=0)

    ar = jnp.concatenate([a_t, r_t], axis=0)
    bk = jnp.concatenate([bd(b_t), bd(k_t)], axis=0)
    m = lax.dot_general(ar, bk, nt, **lp)
    pc = nh * c
    tp = lax.broadcasted_iota(jnp.int32, (c, pc), 0)
    sp = lax.broadcasted_iota(jnp.int32, (c, pc), 1) & (c - 1)
    strict = (sp < tp) if fwd else (sp > tp)
    incl = (sp <= tp) if fwd else (sp >= tp)
    n1 = jnp.where(strict, m[:c, :pc], 0.0)
    m_ak = jnp.where(strict, m[:c, pc:], 0.0)
    m_rb = jnp.where(incl, m[c:, :pc], 0.0)
    m_rk = jnp.where(incl, m[c:, pc:], 0.0)

    pinv = jnp.where(sp == tp, 1.0, 0.0).astype(F32) + n1
    nk = jnp.dot(n1, bd(n1), **hp)
    steps = int(math.log2(c)) - 2
    for _ in range(steps):
        rr = jnp.dot(jnp.concatenate([pinv, nk], axis=0), bd(nk), **hp)
        pinv = pinv + rr[:c]
        nk = rr[c:]
    pinv = pinv + jnp.dot(pinv, bd(nk), **hp)

    st = st_ref[...]
    xs = jnp.dot(ar, st.astype(BF16), **lp)
    x1 = xs[:c] + jnp.dot(m_ak.astype(BF16), bd(vb), **lp)
    u = jnp.dot(pinv, bd(x1), **hp)
    ub = u.astype(BF16)
    y = xs[c:] + jnp.dot(jnp.concatenate([m_rb, m_rk], axis=1).astype(BF16),
                         jnp.concatenate([bd(ub), bd(vb)], axis=0), **lp)
    ds = lax.dot_general(jnp.concatenate([b_h, k_h], axis=0), jnp.concatenate([ub, vb], axis=0), tn, **lp)
    wcol = jnp.exp(lax.dot_general(lw, jnp.ones((c, LANES), F32), tn, **hp))
    wcol = jnp.concatenate([wcol] * (g // LANES), axis=1)
    rh = lax.broadcasted_iota(jnp.int32, (g, g), 0) >> 6
    ch = lax.broadcasted_iota(jnp.int32, (g, g), 1) >> 6
    st_ref[...] = st * wcol + jnp.where(rh == ch, ds, 0.0)
    return y


def _scan_kernel(rf, kf, vf, kkf, lwf, af, rb, kb, vb, kkb, lwb, ab, ka_ref, yf_o, yb_o, stf, stb):
    @pl.when(pl.program_id(2) == 0)
    def _():
        stf[...] = jnp.zeros_like(stf)
        stb[...] = jnp.zeros_like(stb)

    ka = ka_ref[...]
    yf_o[...] = _scan_chunk(rf[...], kf[...], vf[...], kkf[...], lwf[...], af[...], ka, stf, True)
    yb_o[...] = _scan_chunk(rb[...], kb[...], vb[...], kkb[...], lwb[...], ab[...], ka, stb, False)


def _rwkv_scan(r, k, v, kk, lw0, lw1, a0, a1, k_a, batch, seq):
    t = r.shape[0]
    c = _tile(seq, SCAN_CHUNK)
    g = SCAN_GROUP
    nc = seq // c
    ng = RWKV_WIDTH // g
    fwd = pl.BlockSpec((c, g), lambda b, gi, ci: (b * nc + ci, gi))
    bwd = pl.BlockSpec((c, g), lambda b, gi, ci: (b * nc + nc - 1 - ci, gi))
    return pl.pallas_call(
        _scan_kernel,
        grid=(batch, ng, nc),
        in_specs=[fwd] * 6 + [bwd] * 6 + [pl.BlockSpec((1, g), lambda b, gi, ci: (0, gi))],
        out_specs=[fwd, bwd],
        out_shape=[jax.ShapeDtypeStruct((t, RWKV_WIDTH), F32)] * 2,
        scratch_shapes=[pltpu.VMEM((g, g), F32), pltpu.VMEM((g, g), F32)],
        compiler_params=_cparams(("arbitrary", "arbitrary", "arbitrary")),
        name="rwkv_scan",
    )(r, k, v, kk, lw0, a0, r, k, v, kk, lw1, a1, k_a.reshape(1, RWKV_WIDTH))


def _rwkv_post_kernel(yf, yb, r, k, v, a0, a1, g, ka, rk, gg, gb, o_ref):
    inv = 1.0 / RWKV_HEAD
    y = yf[...] + yb[...]
    mean = _head_sum(y) * inv
    yc = y - mean
    var = _head_sum(yc * yc) * inv
    yn = yc * lax.rsqrt(var + GN_EPS) * gg[...] + gb[...]
    ksum = k[...] * (2.0 + (a0[...] + a1[...] - 2.0) * ka[...])
    bonus = _head_sum(r[...] * ksum * rk[...]) * v[...]
    o_ref[...] = ((yn + bonus) * g[...]).astype(o_ref.dtype)


def _rwkv_post(yf, yb, r, k, v, a0, a1, g, k_a, r_k, gn_gain, gn_bias):
    t, w = yf.shape
    tm = _tile(t, 256)
    blk = pl.BlockSpec((tm, w), lambda i: (i, 0))
    row = pl.BlockSpec((1, w), lambda i: (0, 0))
    return pl.pallas_call(
        _rwkv_post_kernel,
        grid=(t // tm,),
        in_specs=[blk] * 8 + [row] * 4,
        out_specs=blk,
        out_shape=jax.ShapeDtypeStruct((t, w), BF16),
        compiler_params=_cparams(("arbitrary",)),
        name="rwkv_post",
    )(yf, yb, r, k, v, a0, a1, g, k_a.reshape(1, w), r_k.reshape(1, w), gn_gain.reshape(1, w), gn_bias.reshape(1, w))


def _outproj_kernel(yd_ref, yr_ref, wo_ref, x_ref, g1_ref, gain_ref, sc_ref, sh_ref, wr_ref, x_o, hp_o, sc_o):
    half = DIFF_WIDTH
    mix = (jnp.dot(yd_ref[...], wo_ref[:half, :], preferred_element_type=F32)
           + jnp.dot(yr_ref[...], wo_ref[half:, :], preferred_element_type=F32))
    xn = x_ref[...] + g1_ref[0] * mix
    x_o[...] = xn
    hb = _norm_mod(xn, gain_ref[...], sc_ref[0], sh_ref[0]).astype(BF16)
    sc_o[...] = jax.nn.sigmoid(jnp.dot(hb, wr_ref[...], preferred_element_type=F32))
    u = pltpu.bitcast(hb.astype(F32), U32)
    hd = D_MODEL // 2
    hp_o[...] = (u[:, :hd] & jnp.uint32(0xFFFF0000)) | (u[:, hd:] >> 16)


def _out_proj(yd, yr, w_out, x2, g1, gain, sc, sh, w_router, seq):
    t, d = x2.shape
    tm = _tile(seq, 256)
    tps = seq // tm
    mod = pl.BlockSpec((1, 1, d), lambda i: (i // tps, 0, 0))
    return pl.pallas_call(
        _outproj_kernel,
        grid=(t // tm,),
        in_specs=[pl.BlockSpec((tm, DIFF_WIDTH), lambda i: (i, 0)),
                  pl.BlockSpec((tm, RWKV_WIDTH), lambda i: (i, 0)),
                  pl.BlockSpec((d, d), lambda i: (0, 0)),
                  pl.BlockSpec((tm, d), lambda i: (i, 0)),
                  mod,
                  pl.BlockSpec((1, d), lambda i: (0, 0)),
                  mod, mod,
                  pl.BlockSpec((d, N_EXPERTS), lambda i: (0, 0))],
        out_specs=[pl.BlockSpec((tm, d), lambda i: (i, 0)),
                   pl.BlockSpec((tm, d // 2), lambda i: (i, 0)),
                   pl.BlockSpec((tm, N_EXPERTS), lambda i: (i, 0))],
        out_shape=[jax.ShapeDtypeStruct((t, d), F32),
                   jax.ShapeDtypeStruct((t, d // 2), U32),
                   jax.ShapeDtypeStruct((t, N_EXPERTS), F32)],
        compiler_params=_cparams(("arbitrary",)),
        name="out_proj",
    )(yd, yr, w_out, x2, g1, gain.reshape(1, d), sc, sh, w_router)


def _route(scores, router_bias):
    t = scores.shape[0]
    sel = (scores + router_bias.astype(F32)).reshape(t, N_GROUPS, EXPERTS_PER_GROUP)
    group_score = jnp.sum(lax.top_k(sel, TOP_K)[0], axis=-1)
    grp = jnp.argmax(group_score, axis=-1).astype(jnp.int32)
    in_grp = jnp.take_along_axis(sel, grp[:, None, None], axis=1)[:, 0]
    _, local = lax.top_k(in_grp, TOP_K)
    expert = grp[:, None] * EXPERTS_PER_GROUP + local.astype(jnp.int32)
    gate = jnp.take_along_axis(scores, expert, axis=1)
    gate = gate / jnp.sum(gate, axis=-1, keepdims=True)
    flat_e = expert.reshape(t * TOP_K)
    onehot = (flat_e[:, None] == jnp.arange(N_EXPERTS, dtype=jnp.int32)[None, :]).astype(jnp.int32)
    csum = jnp.cumsum(onehot, axis=0)
    rank = jnp.sum((csum - onehot) * onehot, axis=1)
    counts = csum[-1]
    padded = ((counts + MOE_BM - 1) // MOE_BM) * MOE_BM
    padded_ends = jnp.cumsum(padded)
    padded_offsets = padded_ends - padded
    dest = (padded_offsets[flat_e] + rank).astype(jnp.int32)
    nb = (t * TOP_K) // MOE_BM + N_EXPERTS
    block_start = jnp.arange(nb, dtype=jnp.int32) * MOE_BM
    block_e = jnp.minimum(jnp.searchsorted(padded_ends, block_start, side='right'), N_EXPERTS - 1).astype(jnp.int32)
    n_used = (padded_ends[-1] // MOE_BM).astype(jnp.int32).reshape(1)
    return dest, gate, block_e, n_used


def _dispatch_kernel(dest_ref, h_ref, xin_ref, xr_ref, sem):
    del xin_ref
    i = pl.program_id(0)
    tm = h_ref.shape[0]

    def row_copy(r, d):
        return pltpu.make_async_copy(h_ref.at[pl.ds(r, 1), :], xr_ref.at[pl.ds(d, 1), :], sem)

    def issue(r, carry):
        base = (i * tm + r) * TOP_K
        for kx in range(TOP_K):
            row_copy(r, dest_ref[base + kx]).start()
        return carry

    lax.fori_loop(0, tm, issue, 0)

    def drain(r, carry):
        for kx in range(TOP_K):
            row_copy(0, 0).wait()
        return carry

    lax.fori_loop(0, tm, drain, 0)


def _dispatch(hp, dest, n_rows):
    t, hd = hp.shape
    tm = _tile(t, 256)
    zeros = jnp.zeros((n_rows, hd), U32)
    return pl.pallas_call(
        _dispatch_kernel,
        grid_spec=pltpu.PrefetchScalarGridSpec(
            num_scalar_prefetch=1,
            grid=(t // tm,),
            in_specs=[pl.BlockSpec((tm, hd), lambda i, d: (i, 0)),
                      pl.BlockSpec(memory_space=pl.ANY)],
            out_specs=pl.BlockSpec(memory_space=pl.ANY),
            scratch_shapes=[pltpu.SemaphoreType.DMA(())]),
        out_shape=jax.ShapeDtypeStruct((n_rows, hd), U32),
        input_output_aliases={2: 0},
        compiler_params=_cparams(("arbitrary",)),
        name="moe_dispatch",
    )(dest, hp, zeros)


def _ffn_kernel(be_ref, nu_ref, x_ref, wg_ref, wu_ref, wd_ref, o_ref):
    del be_ref
    i = pl.program_id(0)

    @pl.when(i < nu_ref[0])
    def _():
        xp = x_ref[...]
        xa = pltpu.bitcast(xp & jnp.uint32(0xFFFF0000), F32).astype(BF16)
        xb = pltpu.bitcast(xp << 16, F32).astype(BF16)
        x = jnp.concatenate([xa, xb], axis=1)
        hg = jnp.dot(x, wg_ref[0], preferred_element_type=F32)
        hu = jnp.dot(x, wu_ref[0], preferred_element_type=F32)
        h = (hg * jax.nn.sigmoid(hg) * hu).astype(BF16)
        o_ref[...] = jnp.dot(h, wd_ref[0], preferred_element_type=F32)

    @pl.when(i >= nu_ref[0])
    def _():
        o_ref[...] = jnp.zeros_like(o_ref)


def _expert_ffn(xr, block_e, n_used, wg, wu, wd):
    p, hd = xr.shape
    d = 2 * hd
    nb = p // MOE_BM
    return pl.pallas_call(
        _ffn_kernel,
        grid_spec=pltpu.PrefetchScalarGridSpec(
            num_scalar_prefetch=2,
            grid=(nb,),
            in_specs=[pl.BlockSpec((MOE_BM, hd), lambda i, be, nu: (i, 0)),
                      pl.BlockSpec((1, d, D_FF), lambda i, be, nu: (be[i], 0, 0)),
                      pl.BlockSpec((1, d, D_FF), lambda i, be, nu: (be[i], 0, 0)),
                      pl.BlockSpec((1, D_FF, d), lambda i, be, nu: (be[i], 0, 0))],
            out_specs=pl.BlockSpec((MOE_BM, d), lambda i, be, nu: (i, 0))),
        out_shape=jax.ShapeDtypeStruct((p, d), F32),
        compiler_params=_cparams(("arbitrary",)),
        name="moe_ffn",
    )(block_e, n_used, xr, wg, wu, wd)


def _combine_kernel(dest_ref, yr_ref, gate_ref, x_ref, g2_ref, fg_ref, o_ref, buf0, buf1, sem, *, final):
    i = pl.program_id(0)
    tm = x_ref.shape[0]
    bufs = (buf0, buf1)

    def row_copy(kx, r, d):
        return pltpu.make_async_copy(yr_ref.at[pl.ds(d, 1), :], bufs[kx].at[pl.ds(r, 1), :], sem)

    def issue(r, carry):
        base = (i * tm + r) * TOP_K
        for kx in range(TOP_K):
            row_copy(kx, r, dest_ref[base + kx]).start()
        return carry

    lax.fori_loop(0, tm, issue, 0)

    def drain(r, carry):
        for kx in range(TOP_K):
            row_copy(kx, 0, 0).wait()
        return carry

    lax.fori_loop(0, tm, drain, 0)

    gate = gate_ref[...]
    y = gate[:, 0:1] * buf0[...] + gate[:, 1:2] * buf1[...]
    xn = x_ref[...] + g2_ref[0] * y
    if final:
        ms = jnp.mean(xn * xn, axis=-1, keepdims=True)
        xn = xn * lax.rsqrt(ms + RMS_EPS) * fg_ref[...]
    o_ref[...] = xn


def _combine(yr, dest, gate, x2, g2, final_gain, seq, final):
    t, d = x2.shape
    tm = _tile(seq, 256)
    tps = seq // tm
    return pl.pallas_call(
        functools.partial(_combine_kernel, final=final),
        grid_spec=pltpu.PrefetchScalarGridSpec(
            num_scalar_prefetch=1,
            grid=(t // tm,),
            in_specs=[pl.BlockSpec(memory_space=pl.ANY),
                      pl.BlockSpec((tm, TOP_K), lambda i, dd: (i, 0)),
                      pl.BlockSpec((tm, d), lambda i, dd: (i, 0)),
                      pl.BlockSpec((1, 1, d), lambda i, dd: (i // tps, 0, 0)),
                      pl.BlockSpec((1, d), lambda i, dd: (0, 0))],
            out_specs=pl.BlockSpec((tm, d), lambda i, dd: (i, 0)),
            scratch_shapes=[pltpu.VMEM((tm, d), F32), pltpu.VMEM((tm, d), F32),
                            pltpu.SemaphoreType.DMA(())]),
        out_shape=jax.ShapeDtypeStruct((t, d), F32),
        compiler_params=_cparams(("arbitrary",)),
        name="moe_combine",
    )(dest, yr, gate, x2, g2, final_gain.reshape(1, d))


def kernel(x, c, positions, w_ada, b_ada, norm_gain, w_in, w_out, mu_prev, mu_next, decay_w0, decay_up, iclr_a0, iclr_up, k_k, k_a, r_k, gate_up, gn_gain, gn_bias, lam_q1, lam_k1, lam_q2, lam_k2, diff_sub_gain, w_router, router_bias, w_gate, w_up, w_down, final_gain):
    batch, seq, d = x.shape
    depth = w_ada.shape[0]
    t = batch * seq
    w = RWKV_WIDTH
    x2 = x.reshape(t, d)
    mod = _ada_mod(c, w_ada, b_ada)
    rope_tabs = _rope_tables(positions)
    wr = w_router.astype(BF16)
    n_rows = t * TOP_K + N_EXPERTS * MOE_BM
    for l in range(depth):
        sh1, sc1, g1, sh2, sc2, g2 = [mod[l, :, i * d:(i + 1) * d].reshape(batch, 1, d) for i in range(N_MOD)]
        w_l = w_in[l].astype(BF16)
        qkv = _project(x2, norm_gain[l, 0], sc1, sh1, w_l[:, :DIFF_COLS], seq, BF16, 1024, rope_tabs)
        p = _project(x2, norm_gain[l, 0], sc1, sh1, w_l[:, DIFF_COLS:], seq, F32, 1152)
        lam_init = 0.8 - 0.6 * math.exp(-0.3 * l)
        lam = (jnp.exp(jnp.sum(lam_q1[l].astype(F32) * lam_k1[l].astype(F32)))
               - jnp.exp(jnp.sum(lam_q2[l].astype(F32) * lam_k2[l].astype(F32))) + lam_init)
        y_diff = _diff_attention(qkv, lam, diff_sub_gain[l], lam_init, batch, seq)
        r, k, v, kk, lw0, lw1, a0, a1, g = _rwkv_prep(p, mu_prev[l], mu_next[l], decay_w0[l], decay_up[l],
                                                      iclr_a0[l], iclr_up[l], k_k[l], gate_up[l], seq)
        yf, yb = _rwkv_scan(r, k, v, kk, lw0, lw1, a0, a1, k_a[l], batch, seq)
        y_rwkv = _rwkv_post(yf, yb, r, k, v, a0, a1, g, k_a[l], r_k[l], gn_gain[l], gn_bias[l])
        x2, hp, scores = _out_proj(y_diff, y_rwkv, w_out[l].astype(BF16), x2, g1, norm_gain[l, 1], sc2, sh2, wr, seq)
        dest, gate, block_e, n_used = _route(scores, router_bias)
        xr = _dispatch(hp, dest, n_rows)
        yr = _expert_ffn(xr, block_e, n_used, w_gate[l].astype(BF16), w_up[l].astype(BF16), w_down[l].astype(BF16))
        x2 = _combine(yr, dest, gate, x2, g2, final_gain, seq, final=(l == depth - 1))
    return x2.reshape(batch, seq, d)
```

```python
import functools
import math

import jax
import jax.numpy as jnp
from jax import lax
from jax.experimental import pallas as pl
from jax.experimental.pallas import tpu as pltpu

F32 = jnp.float32
BF16 = jnp.bfloat16
U32 = jnp.uint32
HIGHEST = lax.Precision.HIGHEST

D_MODEL = 2048
DIFF_WIDTH = 1024
RWKV_WIDTH = 1024
DIFF_QK_DIM = 64
DIFF_V_DIM = 128
DIFF_HEADS = 8
ROPE_DIM = 16
ROPE_THETA = 500000.0
DIFF_NORM_EPS = 1e-5
RWKV_HEAD = 64
DECAY_LORA = 64
ICLR_LORA = 64
GATE_LORA = 128
DECAY_SCALE = 0.606531
GN_EPS = 64e-5
L2_EPS = 1e-12
N_EXPERTS = 16
N_GROUPS = 4
EXPERTS_PER_GROUP = 4
TOP_K = 2
D_FF = 1024
RMS_EPS = 1e-6
N_MOD = 6
DIFF_COLS = 3 * DIFF_WIDTH
RWKV_COLS = 3 * RWKV_WIDTH + 2 * DECAY_LORA + 2 * ICLR_LORA + GATE_LORA
LORA_COLS = RWKV_COLS - 3 * RWKV_WIDTH

LANES = 128
SCAN_CHUNK = 64
SCAN_GROUP = 256
MOE_BM = 256
VMEM_LIMIT = 56 * 1024 * 1024


def _cparams(sem):
    return pltpu.CompilerParams(dimension_semantics=sem, vmem_limit_bytes=VMEM_LIMIT)


def _tile(n, pref):
    t = min(n, pref)
    assert n % t == 0, (n, t)
    return t


def _mod_kernel(c_ref, w_ref, b_ref, o_ref):
    c = c_ref[...]
    ca = (c * jax.nn.sigmoid(c)).astype(BF16)
    o_ref[0] = jnp.dot(ca, w_ref[0].astype(BF16), preferred_element_type=F32) + b_ref[0]


def _ada_mod(c, w_ada, b_ada):
    depth, d, n = w_ada.shape
    b = c.shape[0]
    cp = jnp.zeros((8, d), F32).at[:b].set(c)
    tn = 1024
    out = pl.pallas_call(
        _mod_kernel,
        grid=(depth, n // tn),
        in_specs=[pl.BlockSpec((8, d), lambda l, j: (0, 0)),
                  pl.BlockSpec((1, d, tn), lambda l, j: (l, 0, j)),
                  pl.BlockSpec((1, 1, tn), lambda l, j: (l, 0, j))],
        out_specs=pl.BlockSpec((1, 8, tn), lambda l, j: (l, 0, j)),
        out_shape=jax.ShapeDtypeStruct((depth, 8, n), F32),
        compiler_params=_cparams(("arbitrary", "arbitrary")),
        name="ada_mod",
    )(cp, w_ada, b_ada.reshape(depth, 1, n))
    return out[:, :b]


def _rope_kernel(pos_ref, invf_ref, m1_ref, m2_ref, c_ref, s1_ref, s2_ref):
    ang = pos_ref[...] * invf_ref[...]
    s = jnp.sin(ang)
    c_ref[...] = jnp.cos(ang)
    s1_ref[...] = -s * m1_ref[...]
    s2_ref[...] = s * m2_ref[...]


def _rope_tables(positions):
    t = positions.size
    half = ROPE_DIM // 2
    inv_freq = 1.0 / (ROPE_THETA ** (jnp.arange(0, ROPE_DIM, 2, dtype=F32) / ROPE_DIM))
    lane = jnp.arange(LANES) % DIFF_QK_DIM
    m1 = (lane < half).astype(F32)
    m2 = ((lane >= half) & (lane < ROPE_DIM)).astype(F32)
    invf = jnp.where(lane < ROPE_DIM, inv_freq[lane % half], 0.0).astype(F32)
    pos = jnp.broadcast_to(positions.astype(F32).reshape(t, 1), (t, LANES))
    tm = _tile(t, 512)
    row = pl.BlockSpec((1, LANES), lambda i: (0, 0))
    blk = pl.BlockSpec((tm, LANES), lambda i: (i, 0))
    return pl.pallas_call(
        _rope_kernel,
        grid=(t // tm,),
        in_specs=[blk, row, row, row],
        out_specs=[blk, blk, blk],
        out_shape=[jax.ShapeDtypeStruct((t, LANES), F32)] * 3,
        compiler_params=_cparams(("arbitrary",)),
        name="rope_tables",
    )(pos, invf.reshape(1, LANES), m1.reshape(1, LANES), m2.reshape(1, LANES))


def _norm_mod(x, gain, sc, sh):
    ms = jnp.mean(x * x, axis=-1, keepdims=True)
    return (x * lax.rsqrt(ms + RMS_EPS) * gain) * (1.0 + sc) + sh


def _proj_kernel(x_ref, gain_ref, sc_ref, sh_ref, w_ref, *rest, rope):
    if rope:
        c_ref, s1_ref, s2_ref, o_ref, h_ref = rest
    else:
        o_ref, h_ref = rest
    j = pl.program_id(1)

    @pl.when(j == 0)
    def _():
        h_ref[...] = _norm_mod(x_ref[...], gain_ref[...], sc_ref[0], sh_ref[0]).astype(BF16)

    y = jnp.dot(h_ref[...], w_ref[...], preferred_element_type=F32)
    if not rope:
        o_ref[...] = y.astype(o_ref.dtype)
        return

    @pl.when(j < 2)
    def _():
        scale = jnp.where(j == 0, DIFF_QK_DIM ** -0.5 * math.log2(math.e), 1.0).astype(F32)
        c = c_ref[...]
        s1 = s1_ref[...]
        s2 = s2_ref[...]
        for s in range(y.shape[1] // LANES):
            ys = y[:, s * LANES:(s + 1) * LANES]
            o = ys * c + pltpu.roll(ys, LANES - ROPE_DIM // 2, 1) * s1 + pltpu.roll(ys, ROPE_DIM // 2, 1) * s2
            o_ref[:, s * LANES:(s + 1) * LANES] = (o * scale).astype(o_ref.dtype)

    @pl.when(j == 2)
    def _():
        o_ref[...] = y.astype(o_ref.dtype)


def _project(x2, gain, sc, sh, w, seq, out_dtype, tn, rope_tabs=None):
    t, d = x2.shape
    n = w.shape[1]
    tm = _tile(seq, 512)
    tps = seq // tm
    rope = rope_tabs is not None
    in_specs = [pl.BlockSpec((tm, d), lambda i, j: (i, 0)),
                pl.BlockSpec((1, d), lambda i, j: (0, 0)),
                pl.BlockSpec((1, 1, d), lambda i, j: (i // tps, 0, 0)),
                pl.BlockSpec((1, 1, d), lambda i, j: (i // tps, 0, 0)),
                pl.BlockSpec((d, tn), lambda i, j: (0, j))]
    args = [x2, gain.reshape(1, d), sc, sh, w]
    if rope:
        in_specs += [pl.BlockSpec((tm, LANES), lambda i, j: (i, 0))] * 3
        args += list(rope_tabs)
    return pl.pallas_call(
        functools.partial(_proj_kernel, rope=rope),
        grid=(t // tm, n // tn),
        in_specs=in_specs,
        out_specs=pl.BlockSpec((tm, tn), lambda i, j: (i, j)),
        out_shape=jax.ShapeDtypeStruct((t, n), out_dtype),
        scratch_shapes=[pltpu.VMEM((tm, d), BF16)],
        compiler_params=_cparams(("arbitrary", "arbitrary")),
        name="proj_qkv" if rope else "proj_rwkv",
    )(*args)


def _attn_kernel(q_ref, k_ref, v_ref, lam_ref, gain_ref, o_ref, s0_ref, s1_ref, p0_ref, p1_ref, *, tk, slab, out_scale):
    tq = q_ref.shape[0]
    seq = k_ref.shape[0]
    rows = 2 * tq
    q = q_ref[...]
    lane = lax.broadcasted_iota(jnp.int32, q.shape, 1)
    zero = jnp.zeros_like(q)
    qs = jnp.concatenate([jnp.where(lane < DIFF_QK_DIM, q, zero),
                          jnp.where(lane >= DIFF_QK_DIM, q, zero)], axis=0)
    s_bufs = (s0_ref, s1_ref)
    p_bufs = (p0_ref, p1_ref)

    def scores(c):
        return lax.dot_general(qs, k_ref[c * tk:(c + 1) * tk, :], (((1,), (1,)), ((), ())),
                               preferred_element_type=F32)

    nkv = seq // tk
    m = jnp.full((rows, 1), -1e30, F32)
    l = jnp.zeros((rows, 1), F32)
    acc = jnp.zeros((rows, DIFF_V_DIM), F32)
    s_bufs[0][...] = scores(0)
    for c in range(nkv):
        cur = c % 2
        if c + 1 < nkv:
            s_bufs[1 - cur][...] = scores(c + 1)
        m_parts, l_parts, a_parts = [], [], []
        for r0 in range(0, rows, slab):
            rs = slice(r0, r0 + slab)
            s = s_bufs[cur][rs, :]
            m_new = jnp.maximum(m[rs], jnp.max(s, axis=-1, keepdims=True))
            alpha = jnp.exp2(m[rs] - m_new)
            p = jnp.exp2(s - m_new)
            l_parts.append(alpha * l[rs] + jnp.sum(p, axis=-1, keepdims=True))
            m_parts.append(m_new)
            a_parts.append(alpha)
            p_bufs[cur][rs, :] = p.astype(BF16)
        m = jnp.concatenate(m_parts, axis=0)
        l = jnp.concatenate(l_parts, axis=0)
        acc = (jnp.concatenate(a_parts, axis=0) * acc
               + jnp.dot(p_bufs[cur][...], v_ref[c * tk:(c + 1) * tk, :], preferred_element_type=F32))
    o = acc / l
    d = o[:tq] - lam_ref[...] * o[tq:]
    ms = jnp.mean(d * d, axis=-1, keepdims=True)
    y = (d * lax.rsqrt(ms + DIFF_NORM_EPS) * gain_ref[...]) * out_scale
    o_ref[...] = y.astype(o_ref.dtype)


def _diff_attention(qkv, lam, sub_gain, lam_init, batch, seq):
    t = qkv.shape[0]
    tq = _tile(seq, 256)
    tk = _tile(seq, 1024)
    nq = seq // tq
    h = DIFF_HEADS
    return pl.pallas_call(
        functools.partial(_attn_kernel, tk=tk, slab=32, out_scale=1.0 - lam_init),
        grid=(batch, h, nq),
        scratch_shapes=[pltpu.VMEM((2 * tq, tk), F32), pltpu.VMEM((2 * tq, tk), F32),
                        pltpu.VMEM((2 * tq, tk), BF16), pltpu.VMEM((2 * tq, tk), BF16)],
        in_specs=[pl.BlockSpec((tq, DIFF_V_DIM), lambda b, hh, qi: (b * nq + qi, hh)),
                  pl.BlockSpec((seq, DIFF_V_DIM), lambda b, hh, qi: (b, h + hh)),
                  pl.BlockSpec((seq, DIFF_V_DIM), lambda b, hh, qi: (b, 2 * h + hh)),
                  pl.BlockSpec((1, DIFF_V_DIM), lambda b, hh, qi: (0, 0)),
                  pl.BlockSpec((1, DIFF_V_DIM), lambda b, hh, qi: (0, 0))],
        out_specs=pl.BlockSpec((tq, DIFF_V_DIM), lambda b, hh, qi: (b * nq + qi, hh)),
        out_shape=jax.ShapeDtypeStruct((t, DIFF_WIDTH), BF16),
        compiler_params=_cparams(("arbitrary", "arbitrary", "arbitrary")),
        name="diff_attn",
    )(qkv, qkv, qkv, jnp.broadcast_to(lam.astype(F32).reshape(1, 1), (1, DIFF_V_DIM)),
      sub_gain.astype(F32).reshape(1, DIFF_V_DIM))


def _head_ones(n):
    r = lax.broadcasted_iota(jnp.int32, (n, n), 0) >> 6
    c = lax.broadcasted_iota(jnp.int32, (n, n), 1) >> 6
    return jnp.where(r == c, 1.0, 0.0).astype(BF16)


def _head_sum(x):
    ones = _head_ones(LANES)
    outs = []
    for s in range(x.shape[1] // LANES):
        xs = x[:, s * LANES:(s + 1) * LANES]
        hi = xs.astype(BF16)
        lo = (xs - hi.astype(F32)).astype(BF16)
        outs.append(jnp.dot(hi, ones, preferred_element_type=F32) + jnp.dot(lo, ones, preferred_element_type=F32))
    return jnp.concatenate(outs, axis=1)


def _rwkv_prep_kernel(p_ref, pp_ref, pn_ref, mup_ref, mun_ref, wdec_ref, w0_ref, wa_ref, a0_ref, gup_ref, kkw_ref,
                      r_o, k_o, v_o, kk_o, lw0_o, lw1_o, a0_o, a1_o, g_o, *, tiles_per_seq):
    i = pl.program_id(0)
    tm = p_ref.shape[0]
    w = RWKV_WIDTH
    first = (i % tiles_per_seq) == 0
    last = (i % tiles_per_seq) == tiles_per_seq - 1

    def shifted(c0, c1):
        p = p_ref[:, c0:c1]
        prow = jnp.where(first, 0.0, pp_ref[7:8, c0:c1])
        nrow = jnp.where(last, 0.0, pn_ref[0:1, c0:c1])
        row = lax.broadcasted_iota(jnp.int32, p.shape, 0)
        prev = jnp.where(row == 0, prow, pltpu.roll(p, 1, 0))
        nxt = jnp.where(row == tm - 1, nrow, pltpu.roll(p, tm - 1, 0))
        return p + mup_ref[:, c0:c1] * (prev - p) + mun_ref[:, c0:c1] * (nxt - p)

    lo = shifted(3 * w, 3 * w + LORA_COLS)
    wd = jnp.tanh(lo[:, :2 * DECAY_LORA]).astype(BF16)
    ad = lo[:, 2 * DECAY_LORA:2 * DECAY_LORA + 2 * ICLR_LORA].astype(BF16)
    gd = jax.nn.sigmoid(lo[:, 2 * DECAY_LORA + 2 * ICLR_LORA:]).astype(BF16)
    lw = -DECAY_SCALE * jax.nn.sigmoid(w0_ref[...] + jnp.dot(wd, wdec_ref[...], preferred_element_type=F32))
    lw0_o[...] = lw[:, :w]
    lw1_o[...] = lw[:, w:]
    a = jax.nn.sigmoid(a0_ref[...] + jnp.dot(ad, wa_ref[...], preferred_element_type=F32))
    a0_o[...] = a[:, :w]
    a1_o[...] = a[:, w:]
    g_o[...] = jnp.dot(gd, gup_ref[...], preferred_element_type=F32)
    r_o[...] = shifted(0, w)
    k = shifted(w, 2 * w)
    k_o[...] = k
    v_o[...] = shifted(2 * w, 3 * w)
    kk = k * kkw_ref[...]
    nrm = jnp.sqrt(_head_sum(kk * kk))
    kk_o[...] = kk / jnp.maximum(nrm, L2_EPS)


def _rwkv_prep(p, mu_prev, mu_next, decay_w0, decay_up, iclr_a0, iclr_up, k_k, gate_up, seq):
    t = p.shape[0]
    w = RWKV_WIDTH
    tm = _tile(seq, 256)
    tps = seq // tm
    nblk8 = t // 8
    z = jnp.zeros((DECAY_LORA, w), F32)
    wdec = jnp.concatenate([jnp.concatenate([decay_up[0], z], 0), jnp.concatenate([z, decay_up[1]], 0)], 1).astype(BF16)
    wa = jnp.concatenate([jnp.concatenate([iclr_up[0], z], 0), jnp.concatenate([z, iclr_up[1]], 0)], 1).astype(BF16)
    full = lambda shape: pl.BlockSpec(shape, lambda i: (0,) * len(shape))
    blk = pl.BlockSpec((tm, w), lambda i: (i, 0))
    outs = pl.pallas_call(
        functools.partial(_rwkv_prep_kernel, tiles_per_seq=tps),
        grid=(t // tm,),
        in_specs=[pl.BlockSpec((tm, RWKV_COLS), lambda i: (i, 0)),
                  pl.BlockSpec((8, RWKV_COLS), lambda i: (jnp.maximum(i * (tm // 8) - 1, 0), 0)),
                  pl.BlockSpec((8, RWKV_COLS), lambda i: (jnp.minimum((i + 1) * (tm // 8), nblk8 - 1), 0)),
                  full((1, RWKV_COLS)), full((1, RWKV_COLS)),
                  full((2 * DECAY_LORA, 2 * w)), full((1, 2 * w)),
                  full((2 * ICLR_LORA, 2 * w)), full((1, 2 * w)),
                  full((GATE_LORA, w)), full((1, w))],
        out_specs=[blk] * 9,
        out_shape=[jax.ShapeDtypeStruct((t, w), F32)] * 9,
        compiler_params=_cparams(("arbitrary",)),
        name="rwkv_prep",
    )(p, p, p, mu_prev.reshape(1, -1), mu_next.reshape(1, -1), wdec, decay_w0.reshape(1, 2 * w),
      wa, iclr_a0.reshape(1, 2 * w), gate_up.astype(BF16), k_k.reshape(1, w))
    return outs


def _split2(x):
    hi = x.astype(BF16)
    return hi, (x - hi.astype(F32)).astype(BF16)


def _scan_chunks(chains, ka):
    c, g = chains[0][0].shape
    nh = g // RWKV_HEAD
    pc = nh * c
    lp = dict(preferred_element_type=F32)
    nt = (((1,), (1,)), ((), ()))
    tn = (((0,), (0,)), ((), ()))
    fwds = [ch[7] for ch in chains]
    n = len(chains)
    each = lambda f, *ls: [f(*xs) for xs in zip(*ls)]

    ti = lax.broadcasted_iota(jnp.int32, (c, c), 0)
    si = lax.broadcasted_iota(jnp.int32, (c, c), 1)
    tp = lax.broadcasted_iota(jnp.int32, (c, pc), 0)
    sp = lax.broadcasted_iota(jnp.int32, (c, pc), 1) & (c - 1)
    tri = {f: jnp.where((si <= ti) if f else (si >= ti), 1.0, 0.0).astype(BF16) for f in set(fwds)}
    strict = {f: (sp < tp) if f else (sp > tp) for f in set(fwds)}
    incl = {f: (sp <= tp) if f else (sp >= tp) for f in set(fwds)}
    eye = jnp.where(sp == tp, 1.0, 0.0).astype(F32)
    lane_head = lax.broadcasted_iota(jnp.int32, (c, g), 1) >> 6
    rh = lax.broadcasted_iota(jnp.int32, (g, g), 0) >> 6
    chh = lax.broadcasted_iota(jnp.int32, (g, g), 1) >> 6

    def bd(x):
        z = jnp.zeros_like(x)
        return jnp.concatenate([jnp.where(lane_head == h, x, z) for h in range(nh)], axis=0)

    def mm(x, y):
        return jnp.dot(x.astype(BF16), bd(y.astype(BF16)), **lp)

    r, k, v, kk, lw, a = [[ch[i] for ch in chains] for i in range(6)]
    st_refs = [ch[6] for ch in chains]

    def cum(f, lw_):
        hi, lo = _split2(lw_)
        return jnp.dot(tri[f], hi, **lp) + jnp.dot(tri[f], lo, **lp)

    lc = each(cum, fwds, lw)
    ltot = each(lambda f, x: x[c - 1:c, :] if f else x[0:1, :], fwds, lc)
    e_inc = each(jnp.exp, lc)
    e_inv = each(lambda x: jnp.exp(-x), lc)
    e_rem = each(lambda lt, x: jnp.exp(lt - x), ltot, lc)
    kd = each(lambda k_, a_: k_ * (1.0 + (a_ - 1.0) * ka), k, a)
    b = each(lambda kk_, a_: kk_ * a_, kk, a)
    a_t = each(lambda kk_, lc_, lw_: (-kk_ * jnp.exp(lc_ - lw_)).astype(BF16), kk, lc, lw)
    r_t = each(lambda r_, e: (r_ * e).astype(BF16), r, e_inc)
    b_t = each(lambda b_, e: (b_ * e).astype(BF16), b, e_inv)
    k_t = each(lambda kd_, e: (kd_ * e).astype(BF16), kd, e_inv)
    b_h = each(lambda b_, e: (b_ * e).astype(BF16), b, e_rem)
    k_h = each(lambda kd_, e: (kd_ * e).astype(BF16), kd, e_rem)
    vb = each(lambda v_: v_.astype(BF16), v)

    ar = each(lambda x, y: jnp.concatenate([x, y], axis=0), a_t, r_t)
    m = each(lambda ar_, b_, k_: lax.dot_general(ar_, jnp.concatenate([bd(b_), bd(k_)], axis=0), nt, **lp),
             ar, b_t, k_t)
    n1 = each(lambda f, m_: jnp.where(strict[f], m_[:c, :pc], 0.0), fwds, m)
    m_ak = each(lambda f, m_: jnp.where(strict[f], m_[:c, pc:], 0.0).astype(BF16), fwds, m)
    m_r = each(lambda f, m_: jnp.concatenate([jnp.where(incl[f], m_[c:, :pc], 0.0),
                                              jnp.where(incl[f], m_[c:, pc:], 0.0)], axis=1).astype(BF16), fwds, m)

    pinv = each(lambda x: eye + x, n1)
    nk = each(mm, n1, n1)
    for _ in range(int(math.log2(c)) - 2):
        rr = each(lambda p_, nk_: mm(jnp.concatenate([p_, nk_], axis=0), nk_), pinv, nk)
        pinv = each(lambda p_, rr_: p_ + rr_[:c], pinv, rr)
        nk = [rr_[c:] for rr_ in rr]
    pinv = each(lambda p_, nk_: p_ + mm(p_, nk_), pinv, nk)

    st = [ref[...] for ref in st_refs]
    xs = each(lambda ar_, st_: lax.dot_general(ar_, st_.astype(BF16), nt, **lp), ar, st)
    x1 = each(lambda xs_, mak, vb_: xs_[:c] + jnp.dot(mak, bd(vb_), **lp), xs, m_ak, vb)
    ub = each(lambda p_, x_: mm(p_, x_).astype(BF16), pinv, x1)
    y = each(lambda xs_, mr, ub_, vb_: xs_[c:] + jnp.dot(mr, jnp.concatenate([bd(ub_), bd(vb_)], axis=0), **lp),
             xs, m_r, ub, vb)
    ds = each(lambda ub_, vb_, bh, kh: lax.dot_general(jnp.concatenate([ub_, vb_], axis=0),
                                                       jnp.concatenate([bh, kh], axis=0), tn, **lp),
              ub, vb, b_h, k_h)
    for i in range(n):
        st_refs[i][...] = st[i] * jnp.exp(ltot[i]) + jnp.where(rh == chh, ds[i], 0.0)
    return y


def _scan_kernel(rf, kf, vf, kkf, lwf, af, rb, kb, vb, kkb, lwb, ab, ka_ref, yf_o, yb_o, stf, stb):
    @pl.when(pl.program_id(1) == 0)
    def _():
        stf[...] = jnp.zeros_like(stf)
        stb[...] = jnp.zeros_like(stb)

    nb = rf.shape[0]
    chains = ([(rf[b], kf[b], vf[b], kkf[b], lwf[b], af[b], stf.at[b], True) for b in range(nb)]
              + [(rb[b], kb[b], vb[b], kkb[b], lwb[b], ab[b], stb.at[b], False) for b in range(nb)])
    ys = _scan_chunks(chains, ka_ref[...])
    for b in range(nb):
        yf_o[b] = ys[b]
        yb_o[b] = ys[nb + b]


def _rwkv_scan(r, k, v, kk, lw0, lw1, a0, a1, k_a, batch, seq):
    t, w = r.shape
    c = _tile(seq, SCAN_CHUNK)
    g = SCAN_GROUP
    nc = seq // c
    ng = w // g
    fwd = pl.BlockSpec((batch, c, g), lambda gi, ci: (0, ci, gi))
    bwd = pl.BlockSpec((batch, c, g), lambda gi, ci: (0, nc - 1 - ci, gi))
    b3 = lambda z: z.reshape(batch, seq, w)
    yf, yb = pl.pallas_call(
        _scan_kernel,
        grid=(ng, nc),
        in_specs=[fwd] * 6 + [bwd] * 6 + [pl.BlockSpec((1, g), lambda gi, ci: (0, gi))],
        out_specs=[fwd, bwd],
        out_shape=[jax.ShapeDtypeStruct((batch, seq, w), F32)] * 2,
        scratch_shapes=[pltpu.VMEM((batch, g, g), F32), pltpu.VMEM((batch, g, g), F32)],
        compiler_params=_cparams(("arbitrary", "arbitrary")),
        name="rwkv_scan",
    )(b3(r), b3(k), b3(v), b3(kk), b3(lw0), b3(a0), b3(r), b3(k), b3(v), b3(kk), b3(lw1), b3(a1), k_a.reshape(1, w))
    return yf.reshape(t, w), yb.reshape(t, w)


def _rwkv_post_kernel(yf, yb, r, k, v, a0, a1, g, ka, rk, gg, gb, o_ref):
    inv = 1.0 / RWKV_HEAD
    y = yf[...] + yb[...]
    mean = _head_sum(y) * inv
    yc = y - mean
    var = _head_sum(yc * yc) * inv
    yn = yc * lax.rsqrt(var + GN_EPS) * gg[...] + gb[...]
    ksum = k[...] * (2.0 + (a0[...] + a1[...] - 2.0) * ka[...])
    bonus = _head_sum(r[...] * ksum * rk[...]) * v[...]
    o_ref[...] = ((yn + bonus) * g[...]).astype(o_ref.dtype)


def _rwkv_post(yf, yb, r, k, v, a0, a1, g, k_a, r_k, gn_gain, gn_bias):
    t, w = yf.shape
    tm = _tile(t, 256)
    blk = pl.BlockSpec((tm, w), lambda i: (i, 0))
    row = pl.BlockSpec((1, w), lambda i: (0, 0))
    return pl.pallas_call(
        _rwkv_post_kernel,
        grid=(t // tm,),
        in_specs=[blk] * 8 + [row] * 4,
        out_specs=blk,
        out_shape=jax.ShapeDtypeStruct((t, w), BF16),
        compiler_params=_cparams(("arbitrary",)),
        name="rwkv_post",
    )(yf, yb, r, k, v, a0, a1, g, k_a.reshape(1, w), r_k.reshape(1, w), gn_gain.reshape(1, w), gn_bias.reshape(1, w))


def _route_tile(sc, sel):
    rows = lambda m: [m[e:e + 1, :] for e in range(N_EXPERTS)]
    sel_r = rows(sel)
    sc_r = rows(sc)
    npg = EXPERTS_PER_GROUP

    def top2_sum(a, b, c, d):
        return jnp.maximum(jnp.maximum(jnp.maximum(a + b, a + c), jnp.maximum(a + d, b + c)),
                           jnp.maximum(b + d, c + d))

    best = top2_sum(*sel_r[0:npg])
    grp = jnp.zeros_like(best, dtype=jnp.int32)
    for gi in range(1, N_GROUPS):
        gs = top2_sum(*sel_r[gi * npg:(gi + 1) * npg])
        upd = gs > best
        grp = jnp.where(upd, gi, grp)
        best = jnp.where(upd, gs, best)

    def in_group(r, j):
        out = r[(N_GROUPS - 1) * npg + j]
        for gi in range(N_GROUPS - 2, -1, -1):
            out = jnp.where(grp == gi, r[gi * npg + j], out)
        return out

    v = [in_group(sel_r, j) for j in range(npg)]
    s = [in_group(sc_r, j) for j in range(npg)]
    b1, i1, s1 = v[0], jnp.zeros_like(grp), s[0]
    for j in range(1, npg):
        upd = v[j] > b1
        b1 = jnp.where(upd, v[j], b1)
        i1 = jnp.where(upd, j, i1)
        s1 = jnp.where(upd, s[j], s1)
    neg = jnp.full_like(b1, -jnp.inf)
    b2, i2, s2 = neg, jnp.zeros_like(grp), s[0]
    for j in range(npg):
        upd = jnp.where(i1 == j, neg, v[j]) > b2
        b2 = jnp.where(upd, v[j], b2)
        i2 = jnp.where(upd, j, i2)
        s2 = jnp.where(upd, s[j], s2)
    den = s1 + s2
    return grp * npg + i1, grp * npg + i2, s1 / den, s2 / den


def _outproj_kernel(yd_ref, yr_ref, wo_ref, x_ref, g1_ref, gain_ref, sc_ref, sh_ref, wrt_ref, bias_ref,
                    x_o, hp_o, eid_o, rank_o, gate_o, cnt_o, run_ref):
    @pl.when(pl.program_id(0) == 0)
    def _():
        run_ref[...] = jnp.zeros_like(run_ref)

    tm = x_ref.shape[0]
    half = DIFF_WIDTH
    mix = (jnp.dot(yd_ref[...], wo_ref[:half, :], preferred_element_type=F32)
           + jnp.dot(yr_ref[...], wo_ref[half:, :], preferred_element_type=F32))
    xn = x_ref[...] + g1_ref[0] * mix
    x_o[...] = xn
    hb = _norm_mod(xn, gain_ref[...], sc_ref[0], sh_ref[0]).astype(BF16)
    u = pltpu.bitcast(hb.astype(F32), U32)
    hd = D_MODEL // 2
    hp_o[...] = (u[:, :hd] & jnp.uint32(0xFFFF0000)) | (u[:, hd:] >> 16)

    logits = lax.dot_general(wrt_ref[...], hb, (((1,), (1,)), ((), ())), preferred_element_type=F32)
    sc = jax.nn.sigmoid(logits)
    e1, e2, g1, g2 = _route_tile(sc, sc + bias_ref[...])
    eid_o[0:1, :] = e1
    eid_o[1:2, :] = e2
    eio = lax.broadcasted_iota(jnp.int32, (N_EXPERTS, tm), 0)
    oh1 = eio == e1
    oh2 = eio == e2
    oh = jnp.where(oh1, 1.0, 0.0) + jnp.where(oh2, 1.0, 0.0)
    before = (lax.broadcasted_iota(jnp.int32, (tm, tm), 0) < lax.broadcasted_iota(jnp.int32, (tm, tm), 1))
    base = (jnp.dot(oh.astype(BF16), jnp.where(before, 1.0, 0.0).astype(BF16), preferred_element_type=F32)
            + run_ref[:, 0:1])
    rank_o[0:1, :] = jnp.sum(jnp.where(oh1, base, 0.0), axis=0, keepdims=True).astype(jnp.int32)
    rank_o[1:2, :] = jnp.sum(jnp.where(oh2, base, 0.0), axis=0, keepdims=True).astype(jnp.int32)
    run = run_ref[...] + jnp.sum(oh, axis=1, keepdims=True)
    run_ref[...] = run
    cnt_o[...] = run
    ri = lax.broadcasted_iota(jnp.int32, (LANES, tm), 0)
    gate_o[...] = jnp.transpose(jnp.where(ri == 0, g1, jnp.where(ri == 1, g2, 0.0)))


def _out_proj(yd, yr, w_out, x2, g1, gain, sc, sh, w_router, router_bias, seq):
    t, d = x2.shape
    tm = _tile(seq, 256)
    tps = seq // tm
    mod = pl.BlockSpec((1, 1, d), lambda i: (i // tps, 0, 0))
    ne = N_EXPERTS
    return pl.pallas_call(
        _outproj_kernel,
        grid=(t // tm,),
        in_specs=[pl.BlockSpec((tm, DIFF_WIDTH), lambda i: (i, 0)),
                  pl.BlockSpec((tm, RWKV_WIDTH), lambda i: (i, 0)),
                  pl.BlockSpec((d, d), lambda i: (0, 0)),
                  pl.BlockSpec((tm, d), lambda i: (i, 0)),
                  mod,
                  pl.BlockSpec((1, d), lambda i: (0, 0)),
                  mod, mod,
                  pl.BlockSpec((ne, d), lambda i: (0, 0)),
                  pl.BlockSpec((ne, 1), lambda i: (0, 0))],
        out_specs=[pl.BlockSpec((tm, d), lambda i: (i, 0)),
                   pl.BlockSpec((tm, d // 2), lambda i: (i, 0)),
                   pl.BlockSpec((TOP_K, tm), lambda i: (0, i)),
                   pl.BlockSpec((TOP_K, tm), lambda i: (0, i)),
                   pl.BlockSpec((tm, LANES), lambda i: (i, 0)),
                   pl.BlockSpec((ne, LANES), lambda i: (0, 0))],
        out_shape=[jax.ShapeDtypeStruct((t, d), F32),
                   jax.ShapeDtypeStruct((t, d // 2), U32),
                   jax.ShapeDtypeStruct((TOP_K, t), jnp.int32),
                   jax.ShapeDtypeStruct((TOP_K, t), jnp.int32),
                   jax.ShapeDtypeStruct((t, LANES), F32),
                   jax.ShapeDtypeStruct((ne, LANES), F32)],
        scratch_shapes=[pltpu.VMEM((ne, LANES), F32)],
        compiler_params=_cparams(("arbitrary",)),
        name="out_proj",
    )(yd, yr, w_out, x2, g1, gain.reshape(1, d), sc, sh, w_router.T.astype(BF16),
      router_bias.astype(F32).reshape(ne, 1))


def _expert_layout(counts, n_blocks):
    counts = counts.astype(jnp.int32)
    padded = ((counts + MOE_BM - 1) // MOE_BM) * MOE_BM
    padded_ends = jnp.cumsum(padded)
    offsets = (padded_ends - padded).astype(jnp.int32)
    block_start = jnp.arange(n_blocks, dtype=jnp.int32) * MOE_BM
    block_e = jnp.minimum(jnp.sum((block_start[:, None] >= padded_ends[None, :]).astype(jnp.int32), axis=1),
                          N_EXPERTS - 1).astype(jnp.int32)
    n_used = (padded_ends[-1] // MOE_BM).astype(jnp.int32).reshape(1)
    return offsets, block_e, n_used


def _dispatch_kernel(eid_ref, rank_ref, off_ref, h_ref, xin_ref, xr_ref, sem):
    del xin_ref
    i = pl.program_id(0)
    tm = h_ref.shape[0]
    t = eid_ref.shape[0] // TOP_K

    def row_copy(r, d):
        return pltpu.make_async_copy(h_ref.at[pl.ds(r, 1), :], xr_ref.at[pl.ds(d, 1), :], sem)

    def issue(r, carry):
        for kx in range(TOP_K):
            a = kx * t + i * tm + r
            row_copy(r, off_ref[eid_ref[a]] + rank_ref[a]).start()
        return carry

    lax.fori_loop(0, tm, issue, 0)

    def drain(r, carry):
        for kx in range(TOP_K):
            row_copy(0, 0).wait()
        return carry

    lax.fori_loop(0, tm, drain, 0)


def _dispatch(hp, eid, rank, offsets, n_rows):
    t, hd = hp.shape
    tm = _tile(t, 256)
    zeros = jnp.zeros((n_rows, hd), U32)
    return pl.pallas_call(
        _dispatch_kernel,
        grid_spec=pltpu.PrefetchScalarGridSpec(
            num_scalar_prefetch=3,
            grid=(t // tm,),
            in_specs=[pl.BlockSpec((tm, hd), lambda i, *_: (i, 0)),
                      pl.BlockSpec(memory_space=pl.ANY)],
            out_specs=pl.BlockSpec(memory_space=pl.ANY),
            scratch_shapes=[pltpu.SemaphoreType.DMA(())]),
        out_shape=jax.ShapeDtypeStruct((n_rows, hd), U32),
        input_output_aliases={4: 0},
        compiler_params=_cparams(("arbitrary",)),
        name="moe_dispatch",
    )(eid, rank, offsets, hp, zeros)


def _ffn_kernel(be_ref, nu_ref, x_ref, wg_ref, wu_ref, wd_ref, o_ref):
    del be_ref
    i = pl.program_id(0)

    @pl.when(i < nu_ref[0])
    def _():
        xp = x_ref[...]
        xa = pltpu.bitcast(xp & jnp.uint32(0xFFFF0000), F32).astype(BF16)
        xb = pltpu.bitcast(xp << 16, F32).astype(BF16)
        x = jnp.concatenate([xa, xb], axis=1)
        hg = jnp.dot(x, wg_ref[0], preferred_element_type=F32)
        hu = jnp.dot(x, wu_ref[0], preferred_element_type=F32)
        h = (hg * jax.nn.sigmoid(hg) * hu).astype(BF16)
        o_ref[...] = jnp.dot(h, wd_ref[0], preferred_element_type=F32)

    @pl.when(i >= nu_ref[0])
    def _():
        o_ref[...] = jnp.zeros_like(o_ref)


def _expert_ffn(xr, block_e, n_used, wg, wu, wd):
    p, hd = xr.shape
    d = 2 * hd
    nb = p // MOE_BM
    return pl.pallas_call(
        _ffn_kernel,
        grid_spec=pltpu.PrefetchScalarGridSpec(
            num_scalar_prefetch=2,
            grid=(nb,),
            in_specs=[pl.BlockSpec((MOE_BM, hd), lambda i, be, nu: (i, 0)),
                      pl.BlockSpec((1, d, D_FF), lambda i, be, nu: (be[i], 0, 0)),
                      pl.BlockSpec((1, d, D_FF), lambda i, be, nu: (be[i], 0, 0)),
                      pl.BlockSpec((1, D_FF, d), lambda i, be, nu: (be[i], 0, 0))],
            out_specs=pl.BlockSpec((MOE_BM, d), lambda i, be, nu: (i, 0))),
        out_shape=jax.ShapeDtypeStruct((p, d), F32),
        compiler_params=_cparams(("arbitrary",)),
        name="moe_ffn",
    )(block_e, n_used, xr, wg, wu, wd)


def _combine_kernel(eid_ref, rank_ref, off_ref, yr_ref, gate_ref, x_ref, g2_ref, fg_ref, o_ref, buf0, buf1, sem, *, final):
    i = pl.program_id(0)
    tm = x_ref.shape[0]
    t = eid_ref.shape[0] // TOP_K
    bufs = (buf0, buf1)

    def row_copy(kx, r, d):
        return pltpu.make_async_copy(yr_ref.at[pl.ds(d, 1), :], bufs[kx].at[pl.ds(r, 1), :], sem)

    def issue(r, carry):
        for kx in range(TOP_K):
            a = kx * t + i * tm + r
            row_copy(kx, r, off_ref[eid_ref[a]] + rank_ref[a]).start()
        return carry

    lax.fori_loop(0, tm, issue, 0)

    def drain(r, carry):
        for kx in range(TOP_K):
            row_copy(kx, 0, 0).wait()
        return carry

    lax.fori_loop(0, tm, drain, 0)

    gate = gate_ref[...]
    y = gate[:, 0:1] * buf0[...] + gate[:, 1:2] * buf1[...]
    xn = x_ref[...] + g2_ref[0] * y
    if final:
        ms = jnp.mean(xn * xn, axis=-1, keepdims=True)
        xn = xn * lax.rsqrt(ms + RMS_EPS) * fg_ref[...]
    o_ref[...] = xn


def _combine(yr, eid, rank, offsets, gate, x2, g2, final_gain, seq, final):
    t, d = x2.shape
    tm = _tile(seq, 256)
    tps = seq // tm
    return pl.pallas_call(
        functools.partial(_combine_kernel, final=final),
        grid_spec=pltpu.PrefetchScalarGridSpec(
            num_scalar_prefetch=3,
            grid=(t // tm,),
            in_specs=[pl.BlockSpec(memory_space=pl.ANY),
                      pl.BlockSpec((tm, LANES), lambda i, *_: (i, 0)),
                      pl.BlockSpec((tm, d), lambda i, *_: (i, 0)),
                      pl.BlockSpec((1, 1, d), lambda i, *_: (i // tps, 0, 0)),
                      pl.BlockSpec((1, d), lambda i, *_: (0, 0))],
            out_specs=pl.BlockSpec((tm, d), lambda i, *_: (i, 0)),
            scratch_shapes=[pltpu.VMEM((tm, d), F32), pltpu.VMEM((tm, d), F32),
                            pltpu.SemaphoreType.DMA(())]),
        out_shape=jax.ShapeDtypeStruct((t, d), F32),
        compiler_params=_cparams(("arbitrary",)),
        name="moe_combine",
    )(eid, rank, offsets, yr, gate, x2, g2, final_gain.reshape(1, d))


def kernel(x, c, positions, w_ada, b_ada, norm_gain, w_in, w_out, mu_prev, mu_next, decay_w0, decay_up, iclr_a0, iclr_up, k_k, k_a, r_k, gate_up, gn_gain, gn_bias, lam_q1, lam_k1, lam_q2, lam_k2, diff_sub_gain, w_router, router_bias, w_gate, w_up, w_down, final_gain):
    batch, seq, d = x.shape
    depth = w_ada.shape[0]
    t = batch * seq
    w = RWKV_WIDTH
    x2 = x.reshape(t, d)
    mod = _ada_mod(c, w_ada, b_ada)
    rope_tabs = _rope_tables(positions)
    n_rows = t * TOP_K + N_EXPERTS * MOE_BM
    for l in range(depth):
        sh1, sc1, g1, sh2, sc2, g2 = [mod[l, :, i * d:(i + 1) * d].reshape(batch, 1, d) for i in range(N_MOD)]
        w_l = w_in[l].astype(BF16)
        qkv = _project(x2, norm_gain[l, 0], sc1, sh1, w_l[:, :DIFF_COLS], seq, BF16, 1024, rope_tabs)
        p = _project(x2, norm_gain[l, 0], sc1, sh1, w_l[:, DIFF_COLS:], seq, F32, 1152)
        lam_init = 0.8 - 0.6 * math.exp(-0.3 * l)
        lam = (jnp.exp(jnp.sum(lam_q1[l].astype(F32) * lam_k1[l].astype(F32)))
               - jnp.exp(jnp.sum(lam_q2[l].astype(F32) * lam_k2[l].astype(F32))) + lam_init)
        y_diff = _diff_attention(qkv, lam, diff_sub_gain[l], lam_init, batch, seq)
        r, k, v, kk, lw0, lw1, a0, a1, g = _rwkv_prep(p, mu_prev[l], mu_next[l], decay_w0[l], decay_up[l],
                                                      iclr_a0[l], iclr_up[l], k_k[l], gate_up[l], seq)
        yf, yb = _rwkv_scan(r, k, v, kk, lw0, lw1, a0, a1, k_a[l], batch, seq)
        y_rwkv = _rwkv_post(yf, yb, r, k, v, a0, a1, g, k_a[l], r_k[l], gn_gain[l], gn_bias[l])
        x2, hp, eid, rank, gate, cnt = _out_proj(y_diff, y_rwkv, w_out[l].astype(BF16), x2, g1, norm_gain[l, 1],
                                                 sc2, sh2, w_router, router_bias, seq)
        offsets, block_e, n_used = _expert_layout(cnt[:, 0], n_rows // MOE_BM)
        eid = eid.reshape(TOP_K * t)
        rank = rank.reshape(TOP_K * t)
        xr = _dispatch(hp, eid, rank, offsets, n_rows)
        yr = _expert_ffn(xr, block_e, n_used, w_gate[l].astype(BF16), w_up[l].astype(BF16), w_down[l].astype(BF16))
        x2 = _combine(yr, eid, rank, offsets, gate, x2, g2, final_gain, seq, final=(l == depth - 1))
    return x2.reshape(batch, seq, d)
```

```python
import functools
import math

import jax
import jax.numpy as jnp
from jax import lax
from jax.experimental import pallas as pl
from jax.experimental.pallas import tpu as pltpu

F32 = jnp.float32
BF16 = jnp.bfloat16
U32 = jnp.uint32
HIGHEST = lax.Precision.HIGHEST

D_MODEL = 2048
DIFF_WIDTH = 1024
RWKV_WIDTH = 1024
DIFF_QK_DIM = 64
DIFF_V_DIM = 128
DIFF_HEADS = 8
ROPE_DIM = 16
ROPE_THETA = 500000.0
DIFF_NORM_EPS = 1e-5
RWKV_HEAD = 64
DECAY_LORA = 64
ICLR_LORA = 64
GATE_LORA = 128
DECAY_SCALE = 0.606531
GN_EPS = 64e-5
L2_EPS = 1e-12
N_EXPERTS = 16
N_GROUPS = 4
EXPERTS_PER_GROUP = 4
TOP_K = 2
D_FF = 1024
RMS_EPS = 1e-6
N_MOD = 6
DIFF_COLS = 3 * DIFF_WIDTH
RWKV_COLS = 3 * RWKV_WIDTH + 2 * DECAY_LORA + 2 * ICLR_LORA + GATE_LORA
LORA_COLS = RWKV_COLS - 3 * RWKV_WIDTH

LANES = 128
SCAN_CHUNK = 64
SCAN_GROUP = 256
MOE_BM = 256
VMEM_LIMIT = 56 * 1024 * 1024


def _cparams(sem):
    return pltpu.CompilerParams(dimension_semantics=sem, vmem_limit_bytes=VMEM_LIMIT)


def _tile(n, pref):
    t = min(n, pref)
    assert n % t == 0, (n, t)
    return t


def _mod_kernel(c_ref, w_ref, b_ref, o_ref):
    c = c_ref[...]
    ca = (c * jax.nn.sigmoid(c)).astype(BF16)
    o_ref[0] = jnp.dot(ca, w_ref[0].astype(BF16), preferred_element_type=F32) + b_ref[0]


def _ada_mod(c, w_ada, b_ada):
    depth, d, n = w_ada.shape
    b = c.shape[0]
    cp = jnp.zeros((8, d), F32).at[:b].set(c)
    tn = 1024
    out = pl.pallas_call(
        _mod_kernel,
        grid=(depth, n // tn),
        in_specs=[pl.BlockSpec((8, d), lambda l, j: (0, 0)),
                  pl.BlockSpec((1, d, tn), lambda l, j: (l, 0, j)),
                  pl.BlockSpec((1, 1, tn), lambda l, j: (l, 0, j))],
        out_specs=pl.BlockSpec((1, 8, tn), lambda l, j: (l, 0, j)),
        out_shape=jax.ShapeDtypeStruct((depth, 8, n), F32),
        compiler_params=_cparams(("arbitrary", "arbitrary")),
        name="ada_mod",
    )(cp, w_ada, b_ada.reshape(depth, 1, n))
    return out[:, :b]


def _rope_kernel(pos_ref, invf_ref, m1_ref, m2_ref, c_ref, s1_ref, s2_ref):
    ang = pos_ref[...] * invf_ref[...]
    s = jnp.sin(ang)
    c_ref[...] = jnp.cos(ang)
    s1_ref[...] = -s * m1_ref[...]
    s2_ref[...] = s * m2_ref[...]


def _rope_tables(positions):
    t = positions.size
    half = ROPE_DIM // 2
    inv_freq = 1.0 / (ROPE_THETA ** (jnp.arange(0, ROPE_DIM, 2, dtype=F32) / ROPE_DIM))
    lane = jnp.arange(LANES) % DIFF_QK_DIM
    m1 = (lane < half).astype(F32)
    m2 = ((lane >= half) & (lane < ROPE_DIM)).astype(F32)
    invf = jnp.where(lane < ROPE_DIM, inv_freq[lane % half], 0.0).astype(F32)
    pos = jnp.broadcast_to(positions.astype(F32).reshape(t, 1), (t, LANES))
    tm = _tile(t, 512)
    row = pl.BlockSpec((1, LANES), lambda i: (0, 0))
    blk = pl.BlockSpec((tm, LANES), lambda i: (i, 0))
    return pl.pallas_call(
        _rope_kernel,
        grid=(t // tm,),
        in_specs=[blk, row, row, row],
        out_specs=[blk, blk, blk],
        out_shape=[jax.ShapeDtypeStruct((t, LANES), F32)] * 3,
        compiler_params=_cparams(("arbitrary",)),
        name="rope_tables",
    )(pos, invf.reshape(1, LANES), m1.reshape(1, LANES), m2.reshape(1, LANES))


def _norm_mod(x, gain, sc, sh):
    ms = jnp.mean(x * x, axis=-1, keepdims=True)
    return (x * lax.rsqrt(ms + RMS_EPS) * gain) * (1.0 + sc) + sh


def _proj_kernel(x_ref, gain_ref, sc_ref, sh_ref, w_ref, o_ref, h_ref):
    @pl.when(pl.program_id(1) == 0)
    def _():
        h_ref[...] = _norm_mod(x_ref[...], gain_ref[...], sc_ref[0], sh_ref[0]).astype(BF16)

    o_ref[...] = jnp.dot(h_ref[...], w_ref[...], preferred_element_type=F32).astype(o_ref.dtype)


def _project(x2, gain, sc, sh, w, seq, out_dtype, tn):
    t, d = x2.shape
    n = w.shape[1]
    tm = _tile(seq, 512)
    tps = seq // tm
    return pl.pallas_call(
        _proj_kernel,
        grid=(t // tm, n // tn),
        in_specs=[pl.BlockSpec((tm, d), lambda i, j: (i, 0)),
                  pl.BlockSpec((1, d), lambda i, j: (0, 0)),
                  pl.BlockSpec((1, 1, d), lambda i, j: (i // tps, 0, 0)),
                  pl.BlockSpec((1, 1, d), lambda i, j: (i // tps, 0, 0)),
                  pl.BlockSpec((d, tn), lambda i, j: (0, j))],
        out_specs=pl.BlockSpec((tm, tn), lambda i, j: (i, j)),
        out_shape=jax.ShapeDtypeStruct((t, n), out_dtype),
        scratch_shapes=[pltpu.VMEM((tm, d), BF16)],
        compiler_params=_cparams(("arbitrary", "arbitrary")),
        name="proj_rwkv",
    )(x2, gain.reshape(1, d), sc, sh, w)


def _rope_t_kernel(pos_ref, invf_ref, c_ref, s_ref):
    ang = invf_ref[...] * pos_ref[...]
    c_ref[...] = jnp.cos(ang)
    s_ref[...] = jnp.sin(ang)


def _rope_tables_t(positions):
    t = positions.size
    half = ROPE_DIM // 2
    inv_freq = (1.0 / (ROPE_THETA ** (jnp.arange(0, ROPE_DIM, 2, dtype=F32) / ROPE_DIM))).reshape(half, 1)
    tm = _tile(t, 2048)
    blk = pl.BlockSpec((half, tm), lambda i: (0, i))
    return pl.pallas_call(
        _rope_t_kernel,
        grid=(t // tm,),
        in_specs=[pl.BlockSpec((1, tm), lambda i: (0, i)), pl.BlockSpec((half, 1), lambda i: (0, 0))],
        out_specs=[blk, blk],
        out_shape=[jax.ShapeDtypeStruct((half, t), F32)] * 2,
        compiler_params=_cparams(("arbitrary",)),
        name="rope_tables_t",
    )(positions.astype(F32).reshape(1, t), inv_freq)


def _qkv_kernel(x_ref, gain_ref, sc_ref, sh_ref, wqt_ref, wk_ref, wvt_ref, c_ref, s1_ref, s2_ref, ct_ref, st_ref,
                qt_o, k_o, vt_o):
    h = _norm_mod(x_ref[...], gain_ref[...], sc_ref[0], sh_ref[0]).astype(BF16)
    nt = (((1,), (1,)), ((), ()))
    half = ROPE_DIM // 2
    yq = lax.dot_general(wqt_ref[...], h, nt, preferred_element_type=F32) * (DIFF_QK_DIM ** -0.5 * math.log2(math.e))
    ct = ct_ref[...]
    st = st_ref[...]
    pieces = []
    for b0 in range(0, yq.shape[0], DIFF_QK_DIM):
        r1 = yq[b0:b0 + half]
        r2 = yq[b0 + half:b0 + ROPE_DIM]
        pieces += [r1 * ct - r2 * st, r2 * ct + r1 * st, yq[b0 + ROPE_DIM:b0 + DIFF_QK_DIM]]
    qt_o[...] = jnp.concatenate(pieces, axis=0).astype(qt_o.dtype)
    vt_o[...] = lax.dot_general(wvt_ref[...], h, nt, preferred_element_type=F32).astype(vt_o.dtype)
    yk = jnp.dot(h, wk_ref[...], preferred_element_type=F32)
    c = c_ref[...]
    s1 = s1_ref[...]
    s2 = s2_ref[...]
    for s in range(yk.shape[1] // LANES):
        ys = yk[:, s * LANES:(s + 1) * LANES]
        o = ys * c + pltpu.roll(ys, LANES - half, 1) * s1 + pltpu.roll(ys, half, 1) * s2
        k_o[:, s * LANES:(s + 1) * LANES] = o.astype(k_o.dtype)


def _project_qkv(x2, gain, sc, sh, wq_t, wk, wv_t, seq, rope_tabs, rope_tabs_t):
    t, d = x2.shape
    n = DIFF_WIDTH
    tm = _tile(seq, 512)
    tps = seq // tm
    half = ROPE_DIM // 2
    whole = lambda shape: pl.BlockSpec(shape, lambda i: (0, 0))
    mod = pl.BlockSpec((1, 1, d), lambda i: (i // tps, 0, 0))
    tab = pl.BlockSpec((tm, LANES), lambda i: (i, 0))
    tab_t = pl.BlockSpec((half, tm), lambda i: (0, i))
    return pl.pallas_call(
        _qkv_kernel,
        grid=(t // tm,),
        in_specs=[pl.BlockSpec((tm, d), lambda i: (i, 0)), whole((1, d)), mod, mod,
                  whole((n, d)), whole((d, n)), whole((n, d)), tab, tab, tab, tab_t, tab_t],
        out_specs=[pl.BlockSpec((n, tm), lambda i: (0, i)),
                   pl.BlockSpec((tm, n), lambda i: (i, 0)),
                   pl.BlockSpec((n, tm), lambda i: (0, i))],
        out_shape=[jax.ShapeDtypeStruct((n, t), BF16), jax.ShapeDtypeStruct((t, n), BF16),
                   jax.ShapeDtypeStruct((n, t), BF16)],
        compiler_params=_cparams(("arbitrary",)),
        name="proj_qkv",
    )(x2, gain.reshape(1, d), sc, sh, wq_t, wk, wv_t, *rope_tabs, *rope_tabs_t)


def _attn_kernel(qt_ref, k_ref, vt_ref, lam_ref, gain_ref, o_ref, *, tk, out_scale):
    tq = qt_ref.shape[1]
    seq = k_ref.shape[0]
    cols = 2 * tq
    q = qt_ref[...]
    row = lax.broadcasted_iota(jnp.int32, q.shape, 0)
    zero = jnp.zeros_like(q)
    qbd = jnp.concatenate([jnp.where(row < DIFF_QK_DIM, q, zero),
                           jnp.where(row >= DIFF_QK_DIM, q, zero)], axis=1)

    def scores(c):
        return jnp.dot(k_ref[c * tk:(c + 1) * tk, :], qbd, preferred_element_type=F32)

    nkv = seq // tk
    m = jnp.full((1, cols), -1e30, F32)
    l = jnp.zeros((1, cols), F32)
    acc = jnp.zeros((DIFF_V_DIM, cols), F32)
    s_next = scores(0)
    for c in range(nkv):
        s = s_next
        if c + 1 < nkv:
            s_next = scores(c + 1)
        m_new = jnp.maximum(m, jnp.max(s, axis=0, keepdims=True))
        alpha = jnp.exp2(m - m_new)
        p = jnp.exp2(s - m_new)
        l = alpha * l + jnp.sum(p, axis=0, keepdims=True)
        acc = alpha * acc + jnp.dot(vt_ref[:, c * tk:(c + 1) * tk], p.astype(BF16), preferred_element_type=F32)
        m = m_new
    o = acc / l
    d = o[:, :tq] - lam_ref[...] * o[:, tq:]
    ms = jnp.mean(d * d, axis=0, keepdims=True)
    y = (d * lax.rsqrt(ms + DIFF_NORM_EPS) * gain_ref[...]) * out_scale
    o_ref[...] = jnp.transpose(y).astype(o_ref.dtype)


def _diff_attention(qt, k, vt, lam, sub_gain, lam_init, batch, seq):
    t = k.shape[0]
    tq = _tile(seq, 256)
    tk = _tile(seq, 1024)
    nq = seq // tq
    dv = DIFF_V_DIM
    return pl.pallas_call(
        functools.partial(_attn_kernel, tk=tk, out_scale=1.0 - lam_init),
        grid=(batch, DIFF_HEADS, nq),
        in_specs=[pl.BlockSpec((dv, tq), lambda b, hh, qi: (hh, b * nq + qi)),
                  pl.BlockSpec((seq, dv), lambda b, hh, qi: (b, hh)),
                  pl.BlockSpec((dv, seq), lambda b, hh, qi: (hh, b)),
                  pl.BlockSpec((dv, 1), lambda b, hh, qi: (0, 0)),
                  pl.BlockSpec((dv, 1), lambda b, hh, qi: (0, 0))],
        out_specs=pl.BlockSpec((tq, dv), lambda b, hh, qi: (b * nq + qi, hh)),
        out_shape=jax.ShapeDtypeStruct((t, DIFF_WIDTH), BF16),
        compiler_params=_cparams(("arbitrary", "arbitrary", "arbitrary")),
        name="diff_attn",
    )(qt, k, vt, jnp.broadcast_to(lam.astype(F32).reshape(1, 1), (dv, 1)),
      sub_gain.astype(F32).reshape(dv, 1))


def _head_ones(n):
    r = lax.broadcasted_iota(jnp.int32, (n, n), 0) >> 6
    c = lax.broadcasted_iota(jnp.int32, (n, n), 1) >> 6
    return jnp.where(r == c, 1.0, 0.0).astype(BF16)


def _head_sum(x):
    ones = _head_ones(LANES)
    outs = []
    for s in range(x.shape[1] // LANES):
        xs = x[:, s * LANES:(s + 1) * LANES]
        hi = xs.astype(BF16)
        lo = (xs - hi.astype(F32)).astype(BF16)
        outs.append(jnp.dot(hi, ones, preferred_element_type=F32) + jnp.dot(lo, ones, preferred_element_type=F32))
    return jnp.concatenate(outs, axis=1)


def _rwkv_prep_kernel(p_ref, pp_ref, pn_ref, mup_ref, mun_ref, wdec_ref, w0_ref, wa_ref, a0_ref, gup_ref, kkw_ref,
                      r_o, k_o, v_o, kk_o, lw0_o, lw1_o, a0_o, a1_o, g_o, *, tiles_per_seq):
    i = pl.program_id(0)
    tm = p_ref.shape[0]
    w = RWKV_WIDTH
    first = (i % tiles_per_seq) == 0
    last = (i % tiles_per_seq) == tiles_per_seq - 1

    def shifted(c0, c1):
        p = p_ref[:, c0:c1]
        prow = jnp.where(first, 0.0, pp_ref[7:8, c0:c1])
        nrow = jnp.where(last, 0.0, pn_ref[0:1, c0:c1])
        row = lax.broadcasted_iota(jnp.int32, p.shape, 0)
        prev = jnp.where(row == 0, prow, pltpu.roll(p, 1, 0))
        nxt = jnp.where(row == tm - 1, nrow, pltpu.roll(p, tm - 1, 0))
        return p + mup_ref[:, c0:c1] * (prev - p) + mun_ref[:, c0:c1] * (nxt - p)

    lo = shifted(3 * w, 3 * w + LORA_COLS)
    wd = jnp.tanh(lo[:, :2 * DECAY_LORA]).astype(BF16)
    ad = lo[:, 2 * DECAY_LORA:2 * DECAY_LORA + 2 * ICLR_LORA].astype(BF16)
    gd = jax.nn.sigmoid(lo[:, 2 * DECAY_LORA + 2 * ICLR_LORA:]).astype(BF16)
    lw = -DECAY_SCALE * jax.nn.sigmoid(w0_ref[...] + jnp.dot(wd, wdec_ref[...], preferred_element_type=F32))
    lw0_o[...] = lw[:, :w]
    lw1_o[...] = lw[:, w:]
    a = jax.nn.sigmoid(a0_ref[...] + jnp.dot(ad, wa_ref[...], preferred_element_type=F32))
    a0_o[...] = a[:, :w]
    a1_o[...] = a[:, w:]
    g_o[...] = jnp.dot(gd, gup_ref[...], preferred_element_type=F32)
    r_o[...] = shifted(0, w)
    k = shifted(w, 2 * w)
    k_o[...] = k
    v_o[...] = shifted(2 * w, 3 * w)
    kk = k * kkw_ref[...]
    nrm = jnp.sqrt(_head_sum(kk * kk))
    kk_o[...] = kk / jnp.maximum(nrm, L2_EPS)


def _rwkv_prep(p, mu_prev, mu_next, decay_w0, decay_up, iclr_a0, iclr_up, k_k, gate_up, seq):
    t = p.shape[0]
    w = RWKV_WIDTH
    tm = _tile(seq, 256)
    tps = seq // tm
    nblk8 = t // 8
    z = jnp.zeros((DECAY_LORA, w), F32)
    wdec = jnp.concatenate([jnp.concatenate([decay_up[0], z], 0), jnp.concatenate([z, decay_up[1]], 0)], 1).astype(BF16)
    wa = jnp.concatenate([jnp.concatenate([iclr_up[0], z], 0), jnp.concatenate([z, iclr_up[1]], 0)], 1).astype(BF16)
    full = lambda shape: pl.BlockSpec(shape, lambda i: (0,) * len(shape))
    blk = pl.BlockSpec((tm, w), lambda i: (i, 0))
    outs = pl.pallas_call(
        functools.partial(_rwkv_prep_kernel, tiles_per_seq=tps),
        grid=(t // tm,),
        in_specs=[pl.BlockSpec((tm, RWKV_COLS), lambda i: (i, 0)),
                  pl.BlockSpec((8, RWKV_COLS), lambda i: (jnp.maximum(i * (tm // 8) - 1, 0), 0)),
                  pl.BlockSpec((8, RWKV_COLS), lambda i: (jnp.minimum((i + 1) * (tm // 8), nblk8 - 1), 0)),
                  full((1, RWKV_COLS)), full((1, RWKV_COLS)),
                  full((2 * DECAY_LORA, 2 * w)), full((1, 2 * w)),
                  full((2 * ICLR_LORA, 2 * w)), full((1, 2 * w)),
                  full((GATE_LORA, w)), full((1, w))],
        out_specs=[blk] * 9,
        out_shape=[jax.ShapeDtypeStruct((t, w), F32)] * 9,
        compiler_params=_cparams(("arbitrary",)),
        name="rwkv_prep",
    )(p, p, p, mu_prev.reshape(1, -1), mu_next.reshape(1, -1), wdec, decay_w0.reshape(1, 2 * w),
      wa, iclr_a0.reshape(1, 2 * w), gate_up.astype(BF16), k_k.reshape(1, w))
    return outs


def _split2(x):
    hi = x.astype(BF16)
    return hi, (x - hi.astype(F32)).astype(BF16)


def _scan_chunks(chains, ka):
    c, g = chains[0][0].shape
    nh = g // RWKV_HEAD
    pc = nh * c
    lp = dict(preferred_element_type=F32)
    nt = (((1,), (1,)), ((), ()))
    tn = (((0,), (0,)), ((), ()))
    fwds = [ch[7] for ch in chains]
    n = len(chains)
    each = lambda f, *ls: [f(*xs) for xs in zip(*ls)]

    ti = lax.broadcasted_iota(jnp.int32, (c, c), 0)
    si = lax.broadcasted_iota(jnp.int32, (c, c), 1)
    tp = lax.broadcasted_iota(jnp.int32, (c, pc), 0)
    sp = lax.broadcasted_iota(jnp.int32, (c, pc), 1) & (c - 1)
    tri = {f: jnp.where((si <= ti) if f else (si >= ti), 1.0, 0.0).astype(BF16) for f in set(fwds)}
    strict = {f: (sp < tp) if f else (sp > tp) for f in set(fwds)}
    incl = {f: (sp <= tp) if f else (sp >= tp) for f in set(fwds)}
    eye = jnp.where(sp == tp, 1.0, 0.0).astype(F32)
    lane_head = lax.broadcasted_iota(jnp.int32, (c, g), 1) >> 6
    rh = lax.broadcasted_iota(jnp.int32, (g, g), 0) >> 6
    chh = lax.broadcasted_iota(jnp.int32, (g, g), 1) >> 6

    def bd(x):
        z = jnp.zeros_like(x)
        return jnp.concatenate([jnp.where(lane_head == h, x, z) for h in range(nh)], axis=0)

    def mm(x, y):
        return jnp.dot(x.astype(BF16), bd(y.astype(BF16)), **lp)

    r, k, v, kk, lw, a = [[ch[i] for ch in chains] for i in range(6)]
    st_refs = [ch[6] for ch in chains]

    def cum(f, lw_):
        hi, lo = _split2(lw_)
        return jnp.dot(tri[f], hi, **lp) + jnp.dot(tri[f], lo, **lp)

    lc = each(cum, fwds, lw)
    ltot = each(lambda f, x: x[c - 1:c, :] if f else x[0:1, :], fwds, lc)
    e_inc = each(jnp.exp, lc)
    e_inv = each(lambda x: jnp.exp(-x), lc)
    e_rem = each(lambda lt, x: jnp.exp(lt - x), ltot, lc)
    kd = each(lambda k_, a_: k_ * (1.0 + (a_ - 1.0) * ka), k, a)
    b = each(lambda kk_, a_: kk_ * a_, kk, a)
    a_t = each(lambda kk_, lc_, lw_: (-kk_ * jnp.exp(lc_ - lw_)).astype(BF16), kk, lc, lw)
    r_t = each(lambda r_, e: (r_ * e).astype(BF16), r, e_inc)
    b_t = each(lambda b_, e: (b_ * e).astype(BF16), b, e_inv)
    k_t = each(lambda kd_, e: (kd_ * e).astype(BF16), kd, e_inv)
    b_h = each(lambda b_, e: (b_ * e).astype(BF16), b, e_rem)
    k_h = each(lambda kd_, e: (kd_ * e).astype(BF16), kd, e_rem)
    vb = each(lambda v_: v_.astype(BF16), v)

    ar = each(lambda x, y: jnp.concatenate([x, y], axis=0), a_t, r_t)
    m = each(lambda ar_, b_, k_: lax.dot_general(ar_, jnp.concatenate([bd(b_), bd(k_)], axis=0), nt, **lp),
             ar, b_t, k_t)
    n1 = each(lambda f, m_: jnp.where(strict[f], m_[:c, :pc], 0.0), fwds, m)
    m_ak = each(lambda f, m_: jnp.where(strict[f], m_[:c, pc:], 0.0).astype(BF16), fwds, m)
    m_r = each(lambda f, m_: jnp.concatenate([jnp.where(incl[f], m_[c:, :pc], 0.0),
                                              jnp.where(incl[f], m_[c:, pc:], 0.0)], axis=1).astype(BF16), fwds, m)

    pinv = each(lambda x: eye + x, n1)
    nk = each(mm, n1, n1)
    for _ in range(int(math.log2(c)) - 2):
        rr = each(lambda p_, nk_: mm(jnp.concatenate([p_, nk_], axis=0), nk_), pinv, nk)
        pinv = each(lambda p_, rr_: p_ + rr_[:c], pinv, rr)
        nk = [rr_[c:] for rr_ in rr]
    pinv = each(lambda p_, nk_: p_ + mm(p_, nk_), pinv, nk)

    st = [ref[...] for ref in st_refs]
    xs = each(lambda ar_, st_: lax.dot_general(ar_, st_.astype(BF16), nt, **lp), ar, st)
    x1 = each(lambda xs_, mak, vb_: xs_[:c] + jnp.dot(mak, bd(vb_), **lp), xs, m_ak, vb)
    ub = each(lambda p_, x_: mm(p_, x_).astype(BF16), pinv, x1)
    y = each(lambda xs_, mr, ub_, vb_: xs_[c:] + jnp.dot(mr, jnp.concatenate([bd(ub_), bd(vb_)], axis=0), **lp),
             xs, m_r, ub, vb)
    ds = each(lambda ub_, vb_, bh, kh: lax.dot_general(jnp.concatenate([ub_, vb_], axis=0),
                                                       jnp.concatenate([bh, kh], axis=0), tn, **lp),
              ub, vb, b_h, k_h)
    for i in range(n):
        st_refs[i][...] = st[i] * jnp.exp(ltot[i]) + jnp.where(rh == chh, ds[i], 0.0)
    return y


def _scan_kernel(rf, kf, vf, kkf, lwf, af, rb, kb, vb, kkb, lwb, ab, ka_ref, yf_o, yb_o, stf, stb):
    @pl.when(pl.program_id(1) == 0)
    def _():
        stf[...] = jnp.zeros_like(stf)
        stb[...] = jnp.zeros_like(stb)

    nb = rf.shape[0]
    chains = ([(rf[b], kf[b], vf[b], kkf[b], lwf[b], af[b], stf.at[b], True) for b in range(nb)]
              + [(rb[b], kb[b], vb[b], kkb[b], lwb[b], ab[b], stb.at[b], False) for b in range(nb)])
    ys = _scan_chunks(chains, ka_ref[...])
    for b in range(nb):
        yf_o[b] = ys[b]
        yb_o[b] = ys[nb + b]


def _rwkv_scan(r, k, v, kk, lw0, lw1, a0, a1, k_a, batch, seq):
    t, w = r.shape
    c = _tile(seq, SCAN_CHUNK)
    g = SCAN_GROUP
    nc = seq // c
    ng = w // g
    fwd = pl.BlockSpec((batch, c, g), lambda gi, ci: (0, ci, gi))
    bwd = pl.BlockSpec((batch, c, g), lambda gi, ci: (0, nc - 1 - ci, gi))
    b3 = lambda z: z.reshape(batch, seq, w)
    yf, yb = pl.pallas_call(
        _scan_kernel,
        grid=(ng, nc),
        in_specs=[fwd] * 6 + [bwd] * 6 + [pl.BlockSpec((1, g), lambda gi, ci: (0, gi))],
        out_specs=[fwd, bwd],
        out_shape=[jax.ShapeDtypeStruct((batch, seq, w), F32)] * 2,
        scratch_shapes=[pltpu.VMEM((batch, g, g), F32), pltpu.VMEM((batch, g, g), F32)],
        compiler_params=_cparams(("arbitrary", "arbitrary")),
        name="rwkv_scan",
    )(b3(r), b3(k), b3(v), b3(kk), b3(lw0), b3(a0), b3(r), b3(k), b3(v), b3(kk), b3(lw1), b3(a1), k_a.reshape(1, w))
    return yf.reshape(t, w), yb.reshape(t, w)


def _rwkv_post_kernel(yf, yb, r, k, v, a0, a1, g, ka, rk, gg, gb, o_ref):
    inv = 1.0 / RWKV_HEAD
    y = yf[...] + yb[...]
    mean = _head_sum(y) * inv
    yc = y - mean
    var = _head_sum(yc * yc) * inv
    yn = yc * lax.rsqrt(var + GN_EPS) * gg[...] + gb[...]
    ksum = k[...] * (2.0 + (a0[...] + a1[...] - 2.0) * ka[...])
    bonus = _head_sum(r[...] * ksum * rk[...]) * v[...]
    o_ref[...] = ((yn + bonus) * g[...]).astype(o_ref.dtype)


def _rwkv_post(yf, yb, r, k, v, a0, a1, g, k_a, r_k, gn_gain, gn_bias):
    t, w = yf.shape
    tm = _tile(t, 256)
    blk = pl.BlockSpec((tm, w), lambda i: (i, 0))
    row = pl.BlockSpec((1, w), lambda i: (0, 0))
    return pl.pallas_call(
        _rwkv_post_kernel,
        grid=(t // tm,),
        in_specs=[blk] * 8 + [row] * 4,
        out_specs=blk,
        out_shape=jax.ShapeDtypeStruct((t, w), BF16),
        compiler_params=_cparams(("arbitrary",)),
        name="rwkv_post",
    )(yf, yb, r, k, v, a0, a1, g, k_a.reshape(1, w), r_k.reshape(1, w), gn_gain.reshape(1, w), gn_bias.reshape(1, w))


def _route_tile(sc, sel):
    rows = lambda m: [m[e:e + 1, :] for e in range(N_EXPERTS)]
    sel_r = rows(sel)
    sc_r = rows(sc)
    npg = EXPERTS_PER_GROUP

    def top2_sum(a, b, c, d):
        return jnp.maximum(jnp.maximum(jnp.maximum(a + b, a + c), jnp.maximum(a + d, b + c)),
                           jnp.maximum(b + d, c + d))

    best = top2_sum(*sel_r[0:npg])
    grp = jnp.zeros_like(best, dtype=jnp.int32)
    for gi in range(1, N_GROUPS):
        gs = top2_sum(*sel_r[gi * npg:(gi + 1) * npg])
        upd = gs > best
        grp = jnp.where(upd, gi, grp)
        best = jnp.where(upd, gs, best)

    def in_group(r, j):
        out = r[(N_GROUPS - 1) * npg + j]
        for gi in range(N_GROUPS - 2, -1, -1):
            out = jnp.where(grp == gi, r[gi * npg + j], out)
        return out

    v = [in_group(sel_r, j) for j in range(npg)]
    s = [in_group(sc_r, j) for j in range(npg)]
    b1, i1, s1 = v[0], jnp.zeros_like(grp), s[0]
    for j in range(1, npg):
        upd = v[j] > b1
        b1 = jnp.where(upd, v[j], b1)
        i1 = jnp.where(upd, j, i1)
        s1 = jnp.where(upd, s[j], s1)
    neg = jnp.full_like(b1, -jnp.inf)
    b2, i2, s2 = neg, jnp.zeros_like(grp), s[0]
    for j in range(npg):
        upd = jnp.where(i1 == j, neg, v[j]) > b2
        b2 = jnp.where(upd, v[j], b2)
        i2 = jnp.where(upd, j, i2)
        s2 = jnp.where(upd, s[j], s2)
    den = s1 + s2
    return grp * npg + i1, grp * npg + i2, s1 / den, s2 / den


def _outproj_kernel(yd_ref, yr_ref, wo_ref, x_ref, g1_ref, gain_ref, sc_ref, sh_ref, wrt_ref, bias_ref,
                    x_o, hp_o, eid_o, rank_o, gate_o, cnt_o, run_ref):
    @pl.when(pl.program_id(0) == 0)
    def _():
        run_ref[...] = jnp.zeros_like(run_ref)

    tm = x_ref.shape[0]
    half = DIFF_WIDTH
    mix = (jnp.dot(yd_ref[...], wo_ref[:half, :], preferred_element_type=F32)
           + jnp.dot(yr_ref[...], wo_ref[half:, :], preferred_element_type=F32))
    xn = x_ref[...] + g1_ref[0] * mix
    x_o[...] = xn
    hb = _norm_mod(xn, gain_ref[...], sc_ref[0], sh_ref[0]).astype(BF16)
    u = pltpu.bitcast(hb.astype(F32), U32)
    hd = D_MODEL // 2
    hp_o[...] = (u[:, :hd] & jnp.uint32(0xFFFF0000)) | (u[:, hd:] >> 16)

    logits = lax.dot_general(wrt_ref[...], hb, (((1,), (1,)), ((), ())), preferred_element_type=F32)
    sc = jax.nn.sigmoid(logits)
    e1, e2, g1, g2 = _route_tile(sc, sc + bias_ref[...])
    eid_o[0:1, :] = e1
    eid_o[1:2, :] = e2
    eio = lax.broadcasted_iota(jnp.int32, (N_EXPERTS, tm), 0)
    oh1 = eio == e1
    oh2 = eio == e2
    oh = jnp.where(oh1, 1.0, 0.0) + jnp.where(oh2, 1.0, 0.0)
    before = (lax.broadcasted_iota(jnp.int32, (tm, tm), 0) < lax.broadcasted_iota(jnp.int32, (tm, tm), 1))
    base = (jnp.dot(oh.astype(BF16), jnp.where(before, 1.0, 0.0).astype(BF16), preferred_element_type=F32)
            + run_ref[:, 0:1])
    rank_o[0:1, :] = jnp.sum(jnp.where(oh1, base, 0.0), axis=0, keepdims=True).astype(jnp.int32)
    rank_o[1:2, :] = jnp.sum(jnp.where(oh2, base, 0.0), axis=0, keepdims=True).astype(jnp.int32)
    run = run_ref[...] + jnp.sum(oh, axis=1, keepdims=True)
    run_ref[...] = run
    cnt_o[...] = run
    ri = lax.broadcasted_iota(jnp.int32, (LANES, tm), 0)
    gate_o[...] = jnp.transpose(jnp.where(ri == 0, g1, jnp.where(ri == 1, g2, 0.0)))


def _out_proj(yd, yr, w_out, x2, g1, gain, sc, sh, w_router, router_bias, seq):
    t, d = x2.shape
    tm = _tile(seq, 256)
    tps = seq // tm
    mod = pl.BlockSpec((1, 1, d), lambda i: (i // tps, 0, 0))
    ne = N_EXPERTS
    return pl.pallas_call(
        _outproj_kernel,
        grid=(t // tm,),
        in_specs=[pl.BlockSpec((tm, DIFF_WIDTH), lambda i: (i, 0)),
                  pl.BlockSpec((tm, RWKV_WIDTH), lambda i: (i, 0)),
                  pl.BlockSpec((d, d), lambda i: (0, 0)),
                  pl.BlockSpec((tm, d), lambda i: (i, 0)),
                  mod,
                  pl.BlockSpec((1, d), lambda i: (0, 0)),
                  mod, mod,
                  pl.BlockSpec((ne, d), lambda i: (0, 0)),
                  pl.BlockSpec((ne, 1), lambda i: (0, 0))],
        out_specs=[pl.BlockSpec((tm, d), lambda i: (i, 0)),
                   pl.BlockSpec((tm, d // 2), lambda i: (i, 0)),
                   pl.BlockSpec((TOP_K, tm), lambda i: (0, i)),
                   pl.BlockSpec((TOP_K, tm), lambda i: (0, i)),
                   pl.BlockSpec((tm, LANES), lambda i: (i, 0)),
                   pl.BlockSpec((ne, LANES), lambda i: (0, 0))],
        out_shape=[jax.ShapeDtypeStruct((t, d), F32),
                   jax.ShapeDtypeStruct((t, d // 2), U32),
                   jax.ShapeDtypeStruct((TOP_K, t), jnp.int32),
                   jax.ShapeDtypeStruct((TOP_K, t), jnp.int32),
                   jax.ShapeDtypeStruct((t, LANES), F32),
                   jax.ShapeDtypeStruct((ne, LANES), F32)],
        scratch_shapes=[pltpu.VMEM((ne, LANES), F32)],
        compiler_params=_cparams(("arbitrary",)),
        name="out_proj",
    )(yd, yr, w_out, x2, g1, gain.reshape(1, d), sc, sh, w_router.T.astype(BF16),
      router_bias.astype(F32).reshape(ne, 1))


def _expert_layout(counts, n_blocks):
    counts = counts.astype(jnp.int32)
    padded = ((counts + MOE_BM - 1) // MOE_BM) * MOE_BM
    padded_ends = jnp.cumsum(padded)
    offsets = (padded_ends - padded).astype(jnp.int32)
    block_start = jnp.arange(n_blocks, dtype=jnp.int32) * MOE_BM
    block_e = jnp.minimum(jnp.sum((block_start[:, None] >= padded_ends[None, :]).astype(jnp.int32), axis=1),
                          N_EXPERTS - 1).astype(jnp.int32)
    n_used = (padded_ends[-1] // MOE_BM).astype(jnp.int32).reshape(1)
    return offsets, block_e, n_used


def _dispatch_kernel(eid_ref, rank_ref, off_ref, h_ref, xin_ref, xr_ref, sem, *, tm):
    del xin_ref
    i = pl.program_id(0)
    n = pl.num_programs(0)
    t = eid_ref.shape[0] // TOP_K

    def issue(r, carry):
        tok = i * tm + r
        for kx in range(TOP_K):
            a = kx * t + tok
            d = off_ref[eid_ref[a]] + rank_ref[a]
            pltpu.make_async_copy(h_ref.at[pl.ds(tok, 1), :], xr_ref.at[pl.ds(d, 1), :], sem).start()
        return carry

    lax.fori_loop(0, tm, issue, 0, unroll=8)

    def wait_tile():
        for kx in range(TOP_K):
            pltpu.make_async_copy(h_ref.at[pl.ds(0, tm), :], xr_ref.at[pl.ds(0, tm), :], sem).wait()

    @pl.when(i > 0)
    def _():
        wait_tile()

    @pl.when(i == n - 1)
    def _():
        wait_tile()


def _dispatch(hp, eid, rank, offsets, n_rows):
    t, hd = hp.shape
    tm = _tile(t, 256)
    zeros = jnp.zeros((n_rows, hd), U32)
    return pl.pallas_call(
        functools.partial(_dispatch_kernel, tm=tm),
        grid_spec=pltpu.PrefetchScalarGridSpec(
            num_scalar_prefetch=3,
            grid=(t // tm,),
            in_specs=[pl.BlockSpec(memory_space=pl.ANY),
                      pl.BlockSpec(memory_space=pl.ANY)],
            out_specs=pl.BlockSpec(memory_space=pl.ANY),
            scratch_shapes=[pltpu.SemaphoreType.DMA(())]),
        out_shape=jax.ShapeDtypeStruct((n_rows, hd), U32),
        input_output_aliases={4: 0},
        compiler_params=_cparams(("arbitrary",)),
        name="moe_dispatch",
    )(eid, rank, offsets, hp, zeros)


def _ffn_kernel(be_ref, nu_ref, x_ref, wg_ref, wu_ref, wd_ref, o_ref):
    del be_ref
    i = pl.program_id(0)

    @pl.when(i < nu_ref[0])
    def _():
        xp = x_ref[...]
        xa = pltpu.bitcast(xp & jnp.uint32(0xFFFF0000), F32).astype(BF16)
        xb = pltpu.bitcast(xp << 16, F32).astype(BF16)
        x = jnp.concatenate([xa, xb], axis=1)
        hg = jnp.dot(x, wg_ref[0], preferred_element_type=F32)
        hu = jnp.dot(x, wu_ref[0], preferred_element_type=F32)
        h = (hg * jax.nn.sigmoid(hg) * hu).astype(BF16)
        o_ref[...] = jnp.dot(h, wd_ref[0], preferred_element_type=F32)

    @pl.when(i >= nu_ref[0])
    def _():
        o_ref[...] = jnp.zeros_like(o_ref)


def _expert_ffn(xr, block_e, n_used, wg, wu, wd):
    p, hd = xr.shape
    d = 2 * hd
    nb = p // MOE_BM
    return pl.pallas_call(
        _ffn_kernel,
        grid_spec=pltpu.PrefetchScalarGridSpec(
            num_scalar_prefetch=2,
            grid=(nb,),
            in_specs=[pl.BlockSpec((MOE_BM, hd), lambda i, be, nu: (i, 0)),
                      pl.BlockSpec((1, d, D_FF), lambda i, be, nu: (be[i], 0, 0)),
                      pl.BlockSpec((1, d, D_FF), lambda i, be, nu: (be[i], 0, 0)),
                      pl.BlockSpec((1, D_FF, d), lambda i, be, nu: (be[i], 0, 0))],
            out_specs=pl.BlockSpec((MOE_BM, d), lambda i, be, nu: (i, 0))),
        out_shape=jax.ShapeDtypeStruct((p, d), F32),
        compiler_params=_cparams(("arbitrary",)),
        name="moe_ffn",
    )(block_e, n_used, xr, wg, wu, wd)


def _combine_kernel(eid_ref, rank_ref, off_ref, yr_ref, gate_ref, x_ref, g2_ref, fg_ref, o_ref, buf, sem, *, final):
    i = pl.program_id(0)
    n = pl.num_programs(0)
    tm = x_ref.shape[0]
    t = eid_ref.shape[0] // TOP_K

    def issue_tile(tile, slot):
        def issue(r, carry):
            for kx in range(TOP_K):
                a = kx * t + tile * tm + r
                d = off_ref[eid_ref[a]] + rank_ref[a]
                pltpu.make_async_copy(yr_ref.at[pl.ds(d, 1), :], buf.at[slot, kx, pl.ds(r, 1), :],
                                      sem.at[slot]).start()
            return carry

        lax.fori_loop(0, tm, issue, 0, unroll=8)

    @pl.when(i == 0)
    def _():
        issue_tile(0, 0)

    @pl.when(i + 1 < n)
    def _():
        issue_tile(i + 1, (i + 1) % 2)

    slot = i % 2
    for kx in range(TOP_K):
        pltpu.make_async_copy(yr_ref.at[pl.ds(0, tm), :], buf.at[slot, kx], sem.at[slot]).wait()

    gate = gate_ref[...]
    y = gate[:, 0:1] * buf[slot, 0] + gate[:, 1:2] * buf[slot, 1]
    xn = x_ref[...] + g2_ref[0] * y
    if final:
        ms = jnp.mean(xn * xn, axis=-1, keepdims=True)
        xn = xn * lax.rsqrt(ms + RMS_EPS) * fg_ref[...]
    o_ref[...] = xn


def _combine(yr, eid, rank, offsets, gate, x2, g2, final_gain, seq, final):
    t, d = x2.shape
    tm = _tile(seq, 256)
    tps = seq // tm
    return pl.pallas_call(
        functools.partial(_combine_kernel, final=final),
        grid_spec=pltpu.PrefetchScalarGridSpec(
            num_scalar_prefetch=3,
            grid=(t // tm,),
            in_specs=[pl.BlockSpec(memory_space=pl.ANY),
                      pl.BlockSpec((tm, LANES), lambda i, *_: (i, 0)),
                      pl.BlockSpec((tm, d), lambda i, *_: (i, 0)),
                      pl.BlockSpec((1, 1, d), lambda i, *_: (i // tps, 0, 0)),
                      pl.BlockSpec((1, d), lambda i, *_: (0, 0))],
            out_specs=pl.BlockSpec((tm, d), lambda i, *_: (i, 0)),
            scratch_shapes=[pltpu.VMEM((2, TOP_K, tm, d), F32), pltpu.SemaphoreType.DMA((2,))]),
        out_shape=jax.ShapeDtypeStruct((t, d), F32),
        compiler_params=_cparams(("arbitrary",)),
        name="moe_combine",
    )(eid, rank, offsets, yr, gate, x2, g2, final_gain.reshape(1, d))


def kernel(x, c, positions, w_ada, b_ada, norm_gain, w_in, w_out, mu_prev, mu_next, decay_w0, decay_up, iclr_a0, iclr_up, k_k, k_a, r_k, gate_up, gn_gain, gn_bias, lam_q1, lam_k1, lam_q2, lam_k2, diff_sub_gain, w_router, router_bias, w_gate, w_up, w_down, final_gain):
    batch, seq, d = x.shape
    depth = w_ada.shape[0]
    t = batch * seq
    w = RWKV_WIDTH
    x2 = x.reshape(t, d)
    mod = _ada_mod(c, w_ada, b_ada)
    rope_tabs = _rope_tables(positions)
    rope_tabs_t = _rope_tables_t(positions)
    n_rows = t * TOP_K + N_EXPERTS * MOE_BM
    for l in range(depth):
        sh1, sc1, g1, sh2, sc2, g2 = [mod[l, :, i * d:(i + 1) * d].reshape(batch, 1, d) for i in range(N_MOD)]
        w_l = w_in[l].astype(BF16)
        dw = DIFF_WIDTH
        qt, kq, vt = _project_qkv(x2, norm_gain[l, 0], sc1, sh1, w_l[:, :dw].T, w_l[:, dw:2 * dw],
                                  w_l[:, 2 * dw:3 * dw].T, seq, rope_tabs, rope_tabs_t)
        p = _project(x2, norm_gain[l, 0], sc1, sh1, w_l[:, DIFF_COLS:], seq, F32, 1152)
        lam_init = 0.8 - 0.6 * math.exp(-0.3 * l)
        lam = (jnp.exp(jnp.sum(lam_q1[l].astype(F32) * lam_k1[l].astype(F32)))
               - jnp.exp(jnp.sum(lam_q2[l].astype(F32) * lam_k2[l].astype(F32))) + lam_init)
        y_diff = _diff_attention(qt, kq, vt, lam, diff_sub_gain[l], lam_init, batch, seq)
        r, k, v, kk, lw0, lw1, a0, a1, g = _rwkv_prep(p, mu_prev[l], mu_next[l], decay_w0[l], decay_up[l],
                                                      iclr_a0[l], iclr_up[l], k_k[l], gate_up[l], seq)
        yf, yb = _rwkv_scan(r, k, v, kk, lw0, lw1, a0, a1, k_a[l], batch, seq)
        y_rwkv = _rwkv_post(yf, yb, r, k, v, a0, a1, g, k_a[l], r_k[l], gn_gain[l], gn_bias[l])
        x2, hp, eid, rank, gate, cnt = _out_proj(y_diff, y_rwkv, w_out[l].astype(BF16), x2, g1, norm_gain[l, 1],
                                                 sc2, sh2, w_router, router_bias, seq)
        offsets, block_e, n_used = _expert_layout(cnt[:, 0], n_rows // MOE_BM)
        eid = eid.reshape(TOP_K * t)
        rank = rank.reshape(TOP_K * t)
        xr = _dispatch(hp, eid, rank, offsets, n_rows)
        yr = _expert_ffn(xr, block_e, n_used, w_gate[l].astype(BF16), w_up[l].astype(BF16), w_down[l].astype(BF16))
        x2 = _combine(yr, eid, rank, offsets, gate, x2, g2, final_gain, seq, final=(l == depth - 1))
    return x2.reshape(batch, seq, d)
```

```python
import functools
import math

import jax
import jax.numpy as jnp
from jax import lax
from jax.experimental import pallas as pl
from jax.experimental.pallas import tpu as pltpu

F32 = jnp.float32
BF16 = jnp.bfloat16
U32 = jnp.uint32
HIGHEST = lax.Precision.HIGHEST

D_MODEL = 2048
DIFF_WIDTH = 1024
RWKV_WIDTH = 1024
DIFF_QK_DIM = 64
DIFF_V_DIM = 128
DIFF_HEADS = 8
ROPE_DIM = 16
ROPE_THETA = 500000.0
DIFF_NORM_EPS = 1e-5
RWKV_HEAD = 64
DECAY_LORA = 64
ICLR_LORA = 64
GATE_LORA = 128
DECAY_SCALE = 0.606531
GN_EPS = 64e-5
L2_EPS = 1e-12
N_EXPERTS = 16
N_GROUPS = 4
EXPERTS_PER_GROUP = 4
TOP_K = 2
D_FF = 1024
RMS_EPS = 1e-6
N_MOD = 6
DIFF_COLS = 3 * DIFF_WIDTH
RWKV_COLS = 3 * RWKV_WIDTH + 2 * DECAY_LORA + 2 * ICLR_LORA + GATE_LORA
LORA_COLS = RWKV_COLS - 3 * RWKV_WIDTH

LANES = 128
SCAN_CHUNK = 64
SCAN_GROUP = 256
MOE_BM = 256
VMEM_LIMIT = 56 * 1024 * 1024


def _cparams(sem):
    return pltpu.CompilerParams(dimension_semantics=sem, vmem_limit_bytes=VMEM_LIMIT)


def _tile(n, pref):
    t = min(n, pref)
    assert n % t == 0, (n, t)
    return t


def _mod_kernel(c_ref, w_ref, b_ref, o_ref):
    c = c_ref[...]
    ca = (c * jax.nn.sigmoid(c)).astype(BF16)
    o_ref[0] = jnp.dot(ca, w_ref[0].astype(BF16), preferred_element_type=F32) + b_ref[0]


def _ada_mod(c, w_ada, b_ada):
    depth, d, n = w_ada.shape
    b = c.shape[0]
    cp = jnp.zeros((8, d), F32).at[:b].set(c)
    tn = 1024
    out = pl.pallas_call(
        _mod_kernel,
        grid=(depth, n // tn),
        in_specs=[pl.BlockSpec((8, d), lambda l, j: (0, 0)),
                  pl.BlockSpec((1, d, tn), lambda l, j: (l, 0, j)),
                  pl.BlockSpec((1, 1, tn), lambda l, j: (l, 0, j))],
        out_specs=pl.BlockSpec((1, 8, tn), lambda l, j: (l, 0, j)),
        out_shape=jax.ShapeDtypeStruct((depth, 8, n), F32),
        compiler_params=_cparams(("arbitrary", "arbitrary")),
        name="ada_mod",
    )(cp, w_ada, b_ada.reshape(depth, 1, n))
    return out[:, :b]


def _rope_kernel(pos_ref, invf_ref, m1_ref, m2_ref, c_ref, s1_ref, s2_ref):
    ang = pos_ref[...] * invf_ref[...]
    s = jnp.sin(ang)
    c_ref[...] = jnp.cos(ang)
    s1_ref[...] = -s * m1_ref[...]
    s2_ref[...] = s * m2_ref[...]


def _rope_tables(positions):
    t = positions.size
    half = ROPE_DIM // 2
    inv_freq = 1.0 / (ROPE_THETA ** (jnp.arange(0, ROPE_DIM, 2, dtype=F32) / ROPE_DIM))
    lane = jnp.arange(LANES) % DIFF_QK_DIM
    m1 = (lane < half).astype(F32)
    m2 = ((lane >= half) & (lane < ROPE_DIM)).astype(F32)
    invf = jnp.where(lane < ROPE_DIM, inv_freq[lane % half], 0.0).astype(F32)
    pos = jnp.broadcast_to(positions.astype(F32).reshape(t, 1), (t, LANES))
    tm = _tile(t, 512)
    row = pl.BlockSpec((1, LANES), lambda i: (0, 0))
    blk = pl.BlockSpec((tm, LANES), lambda i: (i, 0))
    return pl.pallas_call(
        _rope_kernel,
        grid=(t // tm,),
        in_specs=[blk, row, row, row],
        out_specs=[blk, blk, blk],
        out_shape=[jax.ShapeDtypeStruct((t, LANES), F32)] * 3,
        compiler_params=_cparams(("arbitrary",)),
        name="rope_tables",
    )(pos, invf.reshape(1, LANES), m1.reshape(1, LANES), m2.reshape(1, LANES))


def _norm_mod(x, gain, sc, sh):
    ms = jnp.mean(x * x, axis=-1, keepdims=True)
    return (x * lax.rsqrt(ms + RMS_EPS) * gain) * (1.0 + sc) + sh


def _proj_kernel(x_ref, gain_ref, sc_ref, sh_ref, w_ref, o_ref, h_ref):
    @pl.when(pl.program_id(1) == 0)
    def _():
        h_ref[...] = _norm_mod(x_ref[...], gain_ref[...], sc_ref[0], sh_ref[0]).astype(BF16)

    o_ref[...] = jnp.dot(h_ref[...], w_ref[...], preferred_element_type=F32).astype(o_ref.dtype)


def _project(x2, gain, sc, sh, w, seq, out_dtype, tn):
    t, d = x2.shape
    n = w.shape[1]
    tm = _tile(seq, 512)
    tps = seq // tm
    return pl.pallas_call(
        _proj_kernel,
        grid=(t // tm, n // tn),
        in_specs=[pl.BlockSpec((tm, d), lambda i, j: (i, 0)),
                  pl.BlockSpec((1, d), lambda i, j: (0, 0)),
                  pl.BlockSpec((1, 1, d), lambda i, j: (i // tps, 0, 0)),
                  pl.BlockSpec((1, 1, d), lambda i, j: (i // tps, 0, 0)),
                  pl.BlockSpec((d, tn), lambda i, j: (0, j))],
        out_specs=pl.BlockSpec((tm, tn), lambda i, j: (i, j)),
        out_shape=jax.ShapeDtypeStruct((t, n), out_dtype),
        scratch_shapes=[pltpu.VMEM((tm, d), BF16)],
        compiler_params=_cparams(("arbitrary", "arbitrary")),
        name="proj_rwkv",
    )(x2, gain.reshape(1, d), sc, sh, w)


def _rope_t_kernel(pos_ref, invf_ref, c_ref, s_ref):
    ang = invf_ref[...] * pos_ref[...]
    c_ref[...] = jnp.cos(ang)
    s_ref[...] = jnp.sin(ang)


def _rope_tables_t(positions):
    t = positions.size
    half = ROPE_DIM // 2
    inv_freq = (1.0 / (ROPE_THETA ** (jnp.arange(0, ROPE_DIM, 2, dtype=F32) / ROPE_DIM))).reshape(half, 1)
    tm = _tile(t, 2048)
    blk = pl.BlockSpec((half, tm), lambda i: (0, i))
    return pl.pallas_call(
        _rope_t_kernel,
        grid=(t // tm,),
        in_specs=[pl.BlockSpec((1, tm), lambda i: (0, i)), pl.BlockSpec((half, 1), lambda i: (0, 0))],
        out_specs=[blk, blk],
        out_shape=[jax.ShapeDtypeStruct((half, t), F32)] * 2,
        compiler_params=_cparams(("arbitrary",)),
        name="rope_tables_t",
    )(positions.astype(F32).reshape(1, t), inv_freq)


def _qkv_kernel(x_ref, gain_ref, sc_ref, sh_ref, wqt_ref, wk_ref, wvt_ref, c_ref, s1_ref, s2_ref, ct_ref, st_ref,
                qt_o, k_o, vt_o):
    h = _norm_mod(x_ref[...], gain_ref[...], sc_ref[0], sh_ref[0]).astype(BF16)
    nt = (((1,), (1,)), ((), ()))
    half = ROPE_DIM // 2
    yq = lax.dot_general(wqt_ref[...], h, nt, preferred_element_type=F32) * (DIFF_QK_DIM ** -0.5 * math.log2(math.e))
    ct = ct_ref[...]
    st = st_ref[...]
    pieces = []
    for b0 in range(0, yq.shape[0], DIFF_QK_DIM):
        r1 = yq[b0:b0 + half]
        r2 = yq[b0 + half:b0 + ROPE_DIM]
        pieces += [r1 * ct - r2 * st, r2 * ct + r1 * st, yq[b0 + ROPE_DIM:b0 + DIFF_QK_DIM]]
    qt_o[...] = jnp.concatenate(pieces, axis=0).astype(qt_o.dtype)
    vt_o[...] = lax.dot_general(wvt_ref[...], h, nt, preferred_element_type=F32).astype(vt_o.dtype)
    yk = jnp.dot(h, wk_ref[...], preferred_element_type=F32)
    c = c_ref[...]
    s1 = s1_ref[...]
    s2 = s2_ref[...]
    for s in range(yk.shape[1] // LANES):
        ys = yk[:, s * LANES:(s + 1) * LANES]
        o = ys * c + pltpu.roll(ys, LANES - half, 1) * s1 + pltpu.roll(ys, half, 1) * s2
        k_o[:, s * LANES:(s + 1) * LANES] = o.astype(k_o.dtype)


def _project_qkv(x2, gain, sc, sh, wq_t, wk, wv_t, seq, rope_tabs, rope_tabs_t):
    t, d = x2.shape
    n = DIFF_WIDTH
    tm = _tile(seq, 512)
    tps = seq // tm
    half = ROPE_DIM // 2
    whole = lambda shape: pl.BlockSpec(shape, lambda i: (0, 0))
    mod = pl.BlockSpec((1, 1, d), lambda i: (i // tps, 0, 0))
    tab = pl.BlockSpec((tm, LANES), lambda i: (i, 0))
    tab_t = pl.BlockSpec((half, tm), lambda i: (0, i))
    return pl.pallas_call(
        _qkv_kernel,
        grid=(t // tm,),
        in_specs=[pl.BlockSpec((tm, d), lambda i: (i, 0)), whole((1, d)), mod, mod,
                  whole((n, d)), whole((d, n)), whole((n, d)), tab, tab, tab, tab_t, tab_t],
        out_specs=[pl.BlockSpec((n, tm), lambda i: (0, i)),
                   pl.BlockSpec((tm, n), lambda i: (i, 0)),
                   pl.BlockSpec((n, tm), lambda i: (0, i))],
        out_shape=[jax.ShapeDtypeStruct((n, t), BF16), jax.ShapeDtypeStruct((t, n), BF16),
                   jax.ShapeDtypeStruct((n, t), BF16)],
        compiler_params=_cparams(("arbitrary",)),
        name="proj_qkv",
    )(x2, gain.reshape(1, d), sc, sh, wq_t, wk, wv_t, *rope_tabs, *rope_tabs_t)


def _attn_kernel(qt_ref, k_ref, vt_ref, lam_ref, gain_ref, o_ref, *, tk, out_scale):
    tq = qt_ref.shape[1]
    seq = k_ref.shape[0]
    cols = 2 * tq
    q = qt_ref[...]
    row = lax.broadcasted_iota(jnp.int32, q.shape, 0)
    zero = jnp.zeros_like(q)
    qbd = jnp.concatenate([jnp.where(row < DIFF_QK_DIM, q, zero),
                           jnp.where(row >= DIFF_QK_DIM, q, zero)], axis=1)

    def scores(c):
        return jnp.dot(k_ref[c * tk:(c + 1) * tk, :], qbd, preferred_element_type=F32)

    nkv = seq // tk
    m = jnp.full((1, cols), -1e30, F32)
    l = jnp.zeros((1, cols), F32)
    acc = jnp.zeros((DIFF_V_DIM, cols), F32)
    s_next = scores(0)
    for c in range(nkv):
        s = s_next
        if c + 1 < nkv:
            s_next = scores(c + 1)
        m_new = jnp.maximum(m, jnp.max(s, axis=0, keepdims=True))
        alpha = jnp.exp2(m - m_new)
        p = jnp.exp2(s - m_new)
        l = alpha * l + jnp.sum(p, axis=0, keepdims=True)
        acc = alpha * acc + jnp.dot(vt_ref[:, c * tk:(c + 1) * tk], p.astype(BF16), preferred_element_type=F32)
        m = m_new
    o = acc / l
    d = o[:, :tq] - lam_ref[...] * o[:, tq:]
    ms = jnp.mean(d * d, axis=0, keepdims=True)
    y = (d * lax.rsqrt(ms + DIFF_NORM_EPS) * gain_ref[...]) * out_scale
    o_ref[...] = jnp.transpose(y).astype(o_ref.dtype)


def _diff_attention(qt, k, vt, lam, sub_gain, lam_init, batch, seq):
    t = k.shape[0]
    tq = _tile(seq, 256)
    tk = _tile(seq, 1024)
    nq = seq // tq
    dv = DIFF_V_DIM
    return pl.pallas_call(
        functools.partial(_attn_kernel, tk=tk, out_scale=1.0 - lam_init),
        grid=(batch, DIFF_HEADS, nq),
        in_specs=[pl.BlockSpec((dv, tq), lambda b, hh, qi: (hh, b * nq + qi)),
                  pl.BlockSpec((seq, dv), lambda b, hh, qi: (b, hh)),
                  pl.BlockSpec((dv, seq), lambda b, hh, qi: (hh, b)),
                  pl.BlockSpec((dv, 1), lambda b, hh, qi: (0, 0)),
                  pl.BlockSpec((dv, 1), lambda b, hh, qi: (0, 0))],
        out_specs=pl.BlockSpec((tq, dv), lambda b, hh, qi: (b * nq + qi, hh)),
        out_shape=jax.ShapeDtypeStruct((t, DIFF_WIDTH), BF16),
        compiler_params=_cparams(("arbitrary", "arbitrary", "arbitrary")),
        name="diff_attn",
    )(qt, k, vt, jnp.broadcast_to(lam.astype(F32).reshape(1, 1), (dv, 1)),
      sub_gain.astype(F32).reshape(dv, 1))


def _head_ones(n):
    r = lax.broadcasted_iota(jnp.int32, (n, n), 0) >> 6
    c = lax.broadcasted_iota(jnp.int32, (n, n), 1) >> 6
    return jnp.where(r == c, 1.0, 0.0).astype(BF16)


def _head_sum(x):
    ones = _head_ones(LANES)
    outs = []
    for s in range(x.shape[1] // LANES):
        xs = x[:, s * LANES:(s + 1) * LANES]
        hi = xs.astype(BF16)
        lo = (xs - hi.astype(F32)).astype(BF16)
        outs.append(jnp.dot(hi, ones, preferred_element_type=F32) + jnp.dot(lo, ones, preferred_element_type=F32))
    return jnp.concatenate(outs, axis=1)


def _rwkv_prep_kernel(p_ref, pp_ref, pn_ref, mup_ref, mun_ref, wdec_ref, w0_ref, wa_ref, a0_ref, gup_ref, kkw_ref,
                      r_o, k_o, v_o, kk_o, lw0_o, lw1_o, a0_o, a1_o, g_o, *, tiles_per_seq):
    i = pl.program_id(0)
    tm = p_ref.shape[0]
    w = RWKV_WIDTH
    first = (i % tiles_per_seq) == 0
    last = (i % tiles_per_seq) == tiles_per_seq - 1

    def shifted(c0, c1):
        p = p_ref[:, c0:c1]
        prow = jnp.where(first, 0.0, pp_ref[7:8, c0:c1])
        nrow = jnp.where(last, 0.0, pn_ref[0:1, c0:c1])
        row = lax.broadcasted_iota(jnp.int32, p.shape, 0)
        prev = jnp.where(row == 0, prow, pltpu.roll(p, 1, 0))
        nxt = jnp.where(row == tm - 1, nrow, pltpu.roll(p, tm - 1, 0))
        return p + mup_ref[:, c0:c1] * (prev - p) + mun_ref[:, c0:c1] * (nxt - p)

    lo = shifted(3 * w, 3 * w + LORA_COLS)
    wd = jnp.tanh(lo[:, :2 * DECAY_LORA]).astype(BF16)
    ad = lo[:, 2 * DECAY_LORA:2 * DECAY_LORA + 2 * ICLR_LORA].astype(BF16)
    gd = jax.nn.sigmoid(lo[:, 2 * DECAY_LORA + 2 * ICLR_LORA:]).astype(BF16)
    lw = -DECAY_SCALE * jax.nn.sigmoid(w0_ref[...] + jnp.dot(wd, wdec_ref[...], preferred_element_type=F32))
    lw0_o[...] = lw[:, :w]
    lw1_o[...] = lw[:, w:]
    a = jax.nn.sigmoid(a0_ref[...] + jnp.dot(ad, wa_ref[...], preferred_element_type=F32))
    a0_o[...] = a[:, :w].astype(a0_o.dtype)
    a1_o[...] = a[:, w:].astype(a1_o.dtype)
    g_o[...] = jnp.dot(gd, gup_ref[...], preferred_element_type=F32).astype(g_o.dtype)
    r_o[...] = shifted(0, w).astype(r_o.dtype)
    k = shifted(w, 2 * w)
    k_o[...] = k.astype(k_o.dtype)
    v_o[...] = shifted(2 * w, 3 * w).astype(v_o.dtype)
    kk = k * kkw_ref[...]
    nrm = jnp.sqrt(_head_sum(kk * kk))
    kk_o[...] = (kk / jnp.maximum(nrm, L2_EPS)).astype(kk_o.dtype)


def _rwkv_prep(p, mu_prev, mu_next, decay_w0, decay_up, iclr_a0, iclr_up, k_k, gate_up, seq):
    t = p.shape[0]
    w = RWKV_WIDTH
    tm = _tile(seq, 256)
    tps = seq // tm
    nblk8 = t // 8
    z = jnp.zeros((DECAY_LORA, w), F32)
    wdec = jnp.concatenate([jnp.concatenate([decay_up[0], z], 0), jnp.concatenate([z, decay_up[1]], 0)], 1).astype(BF16)
    wa = jnp.concatenate([jnp.concatenate([iclr_up[0], z], 0), jnp.concatenate([z, iclr_up[1]], 0)], 1).astype(BF16)
    full = lambda shape: pl.BlockSpec(shape, lambda i: (0,) * len(shape))
    blk = pl.BlockSpec((tm, w), lambda i: (i, 0))
    outs = pl.pallas_call(
        functools.partial(_rwkv_prep_kernel, tiles_per_seq=tps),
        grid=(t // tm,),
        in_specs=[pl.BlockSpec((tm, RWKV_COLS), lambda i: (i, 0)),
                  pl.BlockSpec((8, RWKV_COLS), lambda i: (jnp.maximum(i * (tm // 8) - 1, 0), 0)),
                  pl.BlockSpec((8, RWKV_COLS), lambda i: (jnp.minimum((i + 1) * (tm // 8), nblk8 - 1), 0)),
                  full((1, RWKV_COLS)), full((1, RWKV_COLS)),
                  full((2 * DECAY_LORA, 2 * w)), full((1, 2 * w)),
                  full((2 * ICLR_LORA, 2 * w)), full((1, 2 * w)),
                  full((GATE_LORA, w)), full((1, w))],
        out_specs=[blk] * 9,
        out_shape=([jax.ShapeDtypeStruct((t, w), BF16)] * 4 + [jax.ShapeDtypeStruct((t, w), F32)] * 2
                   + [jax.ShapeDtypeStruct((t, w), BF16)] * 3),
        compiler_params=_cparams(("arbitrary",)),
        name="rwkv_prep",
    )(p, p, p, mu_prev.reshape(1, -1), mu_next.reshape(1, -1), wdec, decay_w0.reshape(1, 2 * w),
      wa, iclr_a0.reshape(1, 2 * w), gate_up.astype(BF16), k_k.reshape(1, w))
    return outs


def _split2(x):
    hi = x.astype(BF16)
    return hi, (x - hi.astype(F32)).astype(BF16)


def _scan_chunks(chains, ka):
    c, g = chains[0][0].shape
    nh = g // RWKV_HEAD
    pc = nh * c
    lp = dict(preferred_element_type=F32)
    nt = (((1,), (1,)), ((), ()))
    tn = (((0,), (0,)), ((), ()))
    fwds = [ch[7] for ch in chains]
    n = len(chains)
    each = lambda f, *ls: [f(*xs) for xs in zip(*ls)]

    ti = lax.broadcasted_iota(jnp.int32, (c, c), 0)
    si = lax.broadcasted_iota(jnp.int32, (c, c), 1)
    tp = lax.broadcasted_iota(jnp.int32, (c, pc), 0)
    sp = lax.broadcasted_iota(jnp.int32, (c, pc), 1) & (c - 1)
    tri = {f: jnp.where((si <= ti) if f else (si >= ti), 1.0, 0.0).astype(BF16) for f in set(fwds)}
    strict = {f: (sp < tp) if f else (sp > tp) for f in set(fwds)}
    incl = {f: (sp <= tp) if f else (sp >= tp) for f in set(fwds)}
    eye = jnp.where(sp == tp, 1.0, 0.0).astype(F32)
    lane_head = lax.broadcasted_iota(jnp.int32, (c, g), 1) >> 6
    rh = lax.broadcasted_iota(jnp.int32, (g, g), 0) >> 6
    chh = lax.broadcasted_iota(jnp.int32, (g, g), 1) >> 6

    def bd(x):
        z = jnp.zeros_like(x)
        return jnp.concatenate([jnp.where(lane_head == h, x, z) for h in range(nh)], axis=0)

    def mm(x, y):
        return jnp.dot(x.astype(BF16), bd(y.astype(BF16)), **lp)

    r, k, v, kk, lw, a = [[ch[i] for ch in chains] for i in range(6)]
    st_refs = [ch[6] for ch in chains]

    def cum(f, lw_):
        hi, lo = _split2(lw_)
        return jnp.dot(tri[f], hi, **lp) + jnp.dot(tri[f], lo, **lp)

    lc = each(cum, fwds, lw)
    ltot = each(lambda f, x: x[c - 1:c, :] if f else x[0:1, :], fwds, lc)
    e_inc = each(jnp.exp, lc)
    e_inv = each(lambda x: jnp.exp(-x), lc)
    e_rem = each(lambda lt, x: jnp.exp(lt - x), ltot, lc)
    kd = each(lambda k_, a_: k_ * (1.0 + (a_ - 1.0) * ka), k, a)
    b = each(lambda kk_, a_: kk_ * a_, kk, a)
    a_t = each(lambda kk_, lc_, lw_: (-kk_ * jnp.exp(lc_ - lw_)).astype(BF16), kk, lc, lw)
    r_t = each(lambda r_, e: (r_ * e).astype(BF16), r, e_inc)
    b_t = each(lambda b_, e: (b_ * e).astype(BF16), b, e_inv)
    k_t = each(lambda kd_, e: (kd_ * e).astype(BF16), kd, e_inv)
    b_h = each(lambda b_, e: (b_ * e).astype(BF16), b, e_rem)
    k_h = each(lambda kd_, e: (kd_ * e).astype(BF16), kd, e_rem)
    vb = each(lambda v_: v_.astype(BF16), v)

    ar = each(lambda x, y: jnp.concatenate([x, y], axis=0), a_t, r_t)
    m = each(lambda ar_, b_, k_: lax.dot_general(ar_, jnp.concatenate([bd(b_), bd(k_)], axis=0), nt, **lp),
             ar, b_t, k_t)
    n1 = each(lambda f, m_: jnp.where(strict[f], m_[:c, :pc], 0.0), fwds, m)
    m_ak = each(lambda f, m_: jnp.where(strict[f], m_[:c, pc:], 0.0).astype(BF16), fwds, m)
    m_r = each(lambda f, m_: jnp.concatenate([jnp.where(incl[f], m_[c:, :pc], 0.0),
                                              jnp.where(incl[f], m_[c:, pc:], 0.0)], axis=1).astype(BF16), fwds, m)

    pinv = each(lambda x: eye + x, n1)
    nk = each(mm, n1, n1)
    for _ in range(int(math.log2(c)) - 2):
        rr = each(lambda p_, nk_: mm(jnp.concatenate([p_, nk_], axis=0), nk_), pinv, nk)
        pinv = each(lambda p_, rr_: p_ + rr_[:c], pinv, rr)
        nk = [rr_[c:] for rr_ in rr]
    pinv = each(lambda p_, nk_: p_ + mm(p_, nk_), pinv, nk)

    st = [ref[...] for ref in st_refs]
    xs = each(lambda ar_, st_: lax.dot_general(ar_, st_.astype(BF16), nt, **lp), ar, st)
    x1 = each(lambda xs_, mak, vb_: xs_[:c] + jnp.dot(mak, bd(vb_), **lp), xs, m_ak, vb)
    ub = each(lambda p_, x_: mm(p_, x_).astype(BF16), pinv, x1)
    y = each(lambda xs_, mr, ub_, vb_: xs_[c:] + jnp.dot(mr, jnp.concatenate([bd(ub_), bd(vb_)], axis=0), **lp),
             xs, m_r, ub, vb)
    ds = each(lambda ub_, vb_, bh, kh: lax.dot_general(jnp.concatenate([ub_, vb_], axis=0),
                                                       jnp.concatenate([bh, kh], axis=0), tn, **lp),
              ub, vb, b_h, k_h)
    for i in range(n):
        st_refs[i][...] = st[i] * jnp.exp(ltot[i]) + jnp.where(rh == chh, ds[i], 0.0)
    return y


def _scan_kernel(rf, kf, vf, kkf, lwf, af, rb, kb, vb, kkb, lwb, ab, ka_ref, yf_o, yb_o, stf, stb):
    @pl.when(pl.program_id(1) == 0)
    def _():
        stf[...] = jnp.zeros_like(stf)
        stb[...] = jnp.zeros_like(stb)

    nb = rf.shape[0]
    ld = lambda ref, b: ref[b].astype(F32)
    chains = ([(ld(rf, b), ld(kf, b), ld(vf, b), ld(kkf, b), lwf[b], ld(af, b), stf.at[b], True) for b in range(nb)]
              + [(ld(rb, b), ld(kb, b), ld(vb, b), ld(kkb, b), lwb[b], ld(ab, b), stb.at[b], False)
                 for b in range(nb)])
    ys = _scan_chunks(chains, ka_ref[...])
    for b in range(nb):
        yf_o[b] = ys[b]
        yb_o[b] = ys[nb + b]


def _rwkv_scan(r, k, v, kk, lw0, lw1, a0, a1, k_a, batch, seq):
    t, w = r.shape
    c = _tile(seq, SCAN_CHUNK)
    g = SCAN_GROUP
    nc = seq // c
    ng = w // g
    fwd = pl.BlockSpec((batch, c, g), lambda gi, ci: (0, ci, gi))
    bwd = pl.BlockSpec((batch, c, g), lambda gi, ci: (0, nc - 1 - ci, gi))
    b3 = lambda z: z.reshape(batch, seq, w)
    yf, yb = pl.pallas_call(
        _scan_kernel,
        grid=(ng, nc),
        in_specs=[fwd] * 6 + [bwd] * 6 + [pl.BlockSpec((1, g), lambda gi, ci: (0, gi))],
        out_specs=[fwd, bwd],
        out_shape=[jax.ShapeDtypeStruct((batch, seq, w), F32)] * 2,
        scratch_shapes=[pltpu.VMEM((batch, g, g), F32), pltpu.VMEM((batch, g, g), F32)],
        compiler_params=_cparams(("arbitrary", "arbitrary")),
        name="rwkv_scan",
    )(b3(r), b3(k), b3(v), b3(kk), b3(lw0), b3(a0), b3(r), b3(k), b3(v), b3(kk), b3(lw1), b3(a1), k_a.reshape(1, w))
    return yf.reshape(t, w), yb.reshape(t, w)


def _rwkv_post_kernel(yf, yb, r, k, v, a0, a1, g, ka, rk, gg, gb, o_ref):
    inv = 1.0 / RWKV_HEAD
    y = yf[...] + yb[...]
    mean = _head_sum(y) * inv
    yc = y - mean
    var = _head_sum(yc * yc) * inv
    yn = yc * lax.rsqrt(var + GN_EPS) * gg[...] + gb[...]
    f = lambda ref: ref[...].astype(F32)
    ksum = f(k) * (2.0 + (f(a0) + f(a1) - 2.0) * ka[...])
    bonus = _head_sum(f(r) * ksum * rk[...]) * f(v)
    o_ref[...] = ((yn + bonus) * f(g)).astype(o_ref.dtype)


def _rwkv_post(yf, yb, r, k, v, a0, a1, g, k_a, r_k, gn_gain, gn_bias):
    t, w = yf.shape
    tm = _tile(t, 256)
    blk = pl.BlockSpec((tm, w), lambda i: (i, 0))
    row = pl.BlockSpec((1, w), lambda i: (0, 0))
    return pl.pallas_call(
        _rwkv_post_kernel,
        grid=(t // tm,),
        in_specs=[blk] * 8 + [row] * 4,
        out_specs=blk,
        out_shape=jax.ShapeDtypeStruct((t, w), BF16),
        compiler_params=_cparams(("arbitrary",)),
        name="rwkv_post",
    )(yf, yb, r, k, v, a0, a1, g, k_a.reshape(1, w), r_k.reshape(1, w), gn_gain.reshape(1, w), gn_bias.reshape(1, w))


def _route_tile(sc, sel):
    rows = lambda m: [m[e:e + 1, :] for e in range(N_EXPERTS)]
    sel_r = rows(sel)
    sc_r = rows(sc)
    npg = EXPERTS_PER_GROUP

    def top2_sum(a, b, c, d):
        return jnp.maximum(jnp.maximum(jnp.maximum(a + b, a + c), jnp.maximum(a + d, b + c)),
                           jnp.maximum(b + d, c + d))

    best = top2_sum(*sel_r[0:npg])
    grp = jnp.zeros_like(best, dtype=jnp.int32)
    for gi in range(1, N_GROUPS):
        gs = top2_sum(*sel_r[gi * npg:(gi + 1) * npg])
        upd = gs > best
        grp = jnp.where(upd, gi, grp)
        best = jnp.where(upd, gs, best)

    def in_group(r, j):
        out = r[(N_GROUPS - 1) * npg + j]
        for gi in range(N_GROUPS - 2, -1, -1):
            out = jnp.where(grp == gi, r[gi * npg + j], out)
        return out

    v = [in_group(sel_r, j) for j in range(npg)]
    s = [in_group(sc_r, j) for j in range(npg)]
    b1, i1, s1 = v[0], jnp.zeros_like(grp), s[0]
    for j in range(1, npg):
        upd = v[j] > b1
        b1 = jnp.where(upd, v[j], b1)
        i1 = jnp.where(upd, j, i1)
        s1 = jnp.where(upd, s[j], s1)
    neg = jnp.full_like(b1, -jnp.inf)
    b2, i2, s2 = neg, jnp.zeros_like(grp), s[0]
    for j in range(npg):
        upd = jnp.where(i1 == j, neg, v[j]) > b2
        b2 = jnp.where(upd, v[j], b2)
        i2 = jnp.where(upd, j, i2)
        s2 = jnp.where(upd, s[j], s2)
    den = s1 + s2
    return grp * npg + i1, grp * npg + i2, s1 / den, s2 / den


def _outproj_kernel(yd_ref, yr_ref, wo_ref, x_ref, g1_ref, gain_ref, sc_ref, sh_ref, wrt_ref, bias_ref,
                    x_o, hp_o, eid_o, rank_o, gate_o, cnt_o, run_ref):
    @pl.when(pl.program_id(0) == 0)
    def _():
        run_ref[...] = jnp.zeros_like(run_ref)

    tm = x_ref.shape[0]
    half = DIFF_WIDTH
    mix = (jnp.dot(yd_ref[...], wo_ref[:half, :], preferred_element_type=F32)
           + jnp.dot(yr_ref[...], wo_ref[half:, :], preferred_element_type=F32))
    xn = x_ref[...] + g1_ref[0] * mix
    x_o[...] = xn
    hb = _norm_mod(xn, gain_ref[...], sc_ref[0], sh_ref[0]).astype(BF16)
    u = pltpu.bitcast(hb.astype(F32), U32)
    hd = D_MODEL // 2
    hp_o[...] = (u[:, :hd] & jnp.uint32(0xFFFF0000)) | (u[:, hd:] >> 16)

    logits = lax.dot_general(wrt_ref[...], hb, (((1,), (1,)), ((), ())), preferred_element_type=F32)
    sc = jax.nn.sigmoid(logits)
    e1, e2, g1, g2 = _route_tile(sc, sc + bias_ref[...])
    eid_o[0:1, :] = e1
    eid_o[1:2, :] = e2
    eio = lax.broadcasted_iota(jnp.int32, (N_EXPERTS, tm), 0)
    oh1 = eio == e1
    oh2 = eio == e2
    oh = jnp.where(oh1, 1.0, 0.0) + jnp.where(oh2, 1.0, 0.0)
    before = (lax.broadcasted_iota(jnp.int32, (tm, tm), 0) < lax.broadcasted_iota(jnp.int32, (tm, tm), 1))
    base = (jnp.dot(oh.astype(BF16), jnp.where(before, 1.0, 0.0).astype(BF16), preferred_element_type=F32)
            + run_ref[:, 0:1])
    rank_o[0:1, :] = jnp.sum(jnp.where(oh1, base, 0.0), axis=0, keepdims=True).astype(jnp.int32)
    rank_o[1:2, :] = jnp.sum(jnp.where(oh2, base, 0.0), axis=0, keepdims=True).astype(jnp.int32)
    run = run_ref[...] + jnp.sum(oh, axis=1, keepdims=True)
    run_ref[...] = run
    cnt_o[...] = run
    ri = lax.broadcasted_iota(jnp.int32, (LANES, tm), 0)
    gate_o[...] = jnp.transpose(jnp.where(ri == 0, g1, jnp.where(ri == 1, g2, 0.0)))


def _out_proj(yd, yr, w_out, x2, g1, gain, sc, sh, w_router, router_bias, seq):
    t, d = x2.shape
    tm = _tile(seq, 256)
    tps = seq // tm
    mod = pl.BlockSpec((1, 1, d), lambda i: (i // tps, 0, 0))
    ne = N_EXPERTS
    return pl.pallas_call(
        _outproj_kernel,
        grid=(t // tm,),
        in_specs=[pl.BlockSpec((tm, DIFF_WIDTH), lambda i: (i, 0)),
                  pl.BlockSpec((tm, RWKV_WIDTH), lambda i: (i, 0)),
                  pl.BlockSpec((d, d), lambda i: (0, 0)),
                  pl.BlockSpec((tm, d), lambda i: (i, 0)),
                  mod,
                  pl.BlockSpec((1, d), lambda i: (0, 0)),
                  mod, mod,
                  pl.BlockSpec((ne, d), lambda i: (0, 0)),
                  pl.BlockSpec((ne, 1), lambda i: (0, 0))],
        out_specs=[pl.BlockSpec((tm, d), lambda i: (i, 0)),
                   pl.BlockSpec((tm, d // 2), lambda i: (i, 0)),
                   pl.BlockSpec((TOP_K, tm), lambda i: (0, i)),
                   pl.BlockSpec((TOP_K, tm), lambda i: (0, i)),
                   pl.BlockSpec((tm, LANES), lambda i: (i, 0)),
                   pl.BlockSpec((ne, LANES), lambda i: (0, 0))],
        out_shape=[jax.ShapeDtypeStruct((t, d), F32),
                   jax.ShapeDtypeStruct((t, d // 2), U32),
                   jax.ShapeDtypeStruct((TOP_K, t), jnp.int32),
                   jax.ShapeDtypeStruct((TOP_K, t), jnp.int32),
                   jax.ShapeDtypeStruct((t, LANES), F32),
                   jax.ShapeDtypeStruct((ne, LANES), F32)],
        scratch_shapes=[pltpu.VMEM((ne, LANES), F32)],
        compiler_params=_cparams(("arbitrary",)),
        name="out_proj",
    )(yd, yr, w_out, x2, g1, gain.reshape(1, d), sc, sh, w_router.T.astype(BF16),
      router_bias.astype(F32).reshape(ne, 1))


def _expert_layout(counts, n_blocks):
    counts = counts.astype(jnp.int32)
    padded = ((counts + MOE_BM - 1) // MOE_BM) * MOE_BM
    padded_ends = jnp.cumsum(padded)
    offsets = (padded_ends - padded).astype(jnp.int32)
    block_start = jnp.arange(n_blocks, dtype=jnp.int32) * MOE_BM
    block_e = jnp.minimum(jnp.sum((block_start[:, None] >= padded_ends[None, :]).astype(jnp.int32), axis=1),
                          N_EXPERTS - 1).astype(jnp.int32)
    n_used = (padded_ends[-1] // MOE_BM).astype(jnp.int32).reshape(1)
    return offsets, block_e, n_used


def _dispatch_kernel(eid_ref, rank_ref, off_ref, h_ref, xin_ref, xr_ref, hbuf, sem_in, sem_out, *, tm):
    del xin_ref
    i = pl.program_id(0)
    n = pl.num_programs(0)
    t = eid_ref.shape[0] // TOP_K
    slot = i % 2

    def load(tile, s):
        return pltpu.make_async_copy(h_ref.at[pl.ds(tile * tm, tm), :], hbuf.at[s], sem_in.at[s])

    def wait_out(s):
        for kx in range(TOP_K):
            pltpu.make_async_copy(hbuf.at[s], xr_ref.at[pl.ds(0, tm), :], sem_out.at[s]).wait()

    @pl.when(i == 0)
    def _():
        load(0, 0).start()

    load(i, slot).wait()

    def issue(r, carry):
        for kx in range(TOP_K):
            a = kx * t + i * tm + r
            d = off_ref[eid_ref[a]] + rank_ref[a]
            pltpu.make_async_copy(hbuf.at[slot, pl.ds(r, 1), :], xr_ref.at[pl.ds(d, 1), :], sem_out.at[slot]).start()
        return carry

    lax.fori_loop(0, tm, issue, 0, unroll=8)

    @pl.when(i > 0)
    def _():
        wait_out(1 - slot)

    @pl.when(i + 1 < n)
    def _():
        load(i + 1, 1 - slot).start()

    @pl.when(i == n - 1)
    def _():
        wait_out(slot)


def _dispatch(hp, eid, rank, offsets, n_rows):
    t, hd = hp.shape
    tm = _tile(t, 256)
    zeros = jnp.zeros((n_rows, hd), U32)
    return pl.pallas_call(
        functools.partial(_dispatch_kernel, tm=tm),
        grid_spec=pltpu.PrefetchScalarGridSpec(
            num_scalar_prefetch=3,
            grid=(t // tm,),
            in_specs=[pl.BlockSpec(memory_space=pl.ANY),
                      pl.BlockSpec(memory_space=pl.ANY)],
            out_specs=pl.BlockSpec(memory_space=pl.ANY),
            scratch_shapes=[pltpu.VMEM((2, tm, hd), U32), pltpu.SemaphoreType.DMA((2,)),
                            pltpu.SemaphoreType.DMA((2,))]),
        out_shape=jax.ShapeDtypeStruct((n_rows, hd), U32),
        input_output_aliases={4: 0},
        compiler_params=_cparams(("arbitrary",)),
        name="moe_dispatch",
    )(eid, rank, offsets, hp, zeros)


def _ffn_kernel(be_ref, nu_ref, x_ref, wg_ref, wu_ref, wd_ref, o_ref):
    del be_ref
    i = pl.program_id(0)

    @pl.when(i < nu_ref[0])
    def _():
        xp = x_ref[...]
        xa = pltpu.bitcast(xp & jnp.uint32(0xFFFF0000), F32).astype(BF16)
        xb = pltpu.bitcast(xp << 16, F32).astype(BF16)
        x = jnp.concatenate([xa, xb], axis=1)
        hg = jnp.dot(x, wg_ref[0], preferred_element_type=F32)
        hu = jnp.dot(x, wu_ref[0], preferred_element_type=F32)
        h = (hg * jax.nn.sigmoid(hg) * hu).astype(BF16)
        o_ref[...] = jnp.dot(h, wd_ref[0], preferred_element_type=F32)

    @pl.when(i >= nu_ref[0])
    def _():
        o_ref[...] = jnp.zeros_like(o_ref)


def _expert_ffn(xr, block_e, n_used, wg, wu, wd):
    p, hd = xr.shape
    d = 2 * hd
    nb = p // MOE_BM
    return pl.pallas_call(
        _ffn_kernel,
        grid_spec=pltpu.PrefetchScalarGridSpec(
            num_scalar_prefetch=2,
            grid=(nb,),
            in_specs=[pl.BlockSpec((MOE_BM, hd), lambda i, be, nu: (i, 0)),
                      pl.BlockSpec((1, d, D_FF), lambda i, be, nu: (be[i], 0, 0)),
                      pl.BlockSpec((1, d, D_FF), lambda i, be, nu: (be[i], 0, 0)),
                      pl.BlockSpec((1, D_FF, d), lambda i, be, nu: (be[i], 0, 0))],
            out_specs=pl.BlockSpec((MOE_BM, d), lambda i, be, nu: (i, 0))),
        out_shape=jax.ShapeDtypeStruct((p, d), F32),
        compiler_params=_cparams(("arbitrary",)),
        name="moe_ffn",
    )(block_e, n_used, xr, wg, wu, wd)


def _combine_kernel(eid_ref, rank_ref, off_ref, yr_ref, gate_ref, x_ref, g2_ref, fg_ref, o_ref, buf, sem, *, final):
    i = pl.program_id(0)
    n = pl.num_programs(0)
    tm = x_ref.shape[0]
    t = eid_ref.shape[0] // TOP_K

    def issue_tile(tile, slot):
        def issue(r, carry):
            for kx in range(TOP_K):
                a = kx * t + tile * tm + r
                d = off_ref[eid_ref[a]] + rank_ref[a]
                pltpu.make_async_copy(yr_ref.at[pl.ds(d, 1), :], buf.at[slot, kx, pl.ds(r, 1), :],
                                      sem.at[slot]).start()
            return carry

        lax.fori_loop(0, tm, issue, 0, unroll=8)

    @pl.when(i == 0)
    def _():
        issue_tile(0, 0)

    @pl.when(i + 1 < n)
    def _():
        issue_tile(i + 1, (i + 1) % 2)

    slot = i % 2
    for kx in range(TOP_K):
        pltpu.make_async_copy(yr_ref.at[pl.ds(0, tm), :], buf.at[slot, kx], sem.at[slot]).wait()

    gate = gate_ref[...]
    y = gate[:, 0:1] * buf[slot, 0] + gate[:, 1:2] * buf[slot, 1]
    xn = x_ref[...] + g2_ref[0] * y
    if final:
        ms = jnp.mean(xn * xn, axis=-1, keepdims=True)
        xn = xn * lax.rsqrt(ms + RMS_EPS) * fg_ref[...]
    o_ref[...] = xn


def _combine(yr, eid, rank, offsets, gate, x2, g2, final_gain, seq, final):
    t, d = x2.shape
    tm = _tile(seq, 256)
    tps = seq // tm
    return pl.pallas_call(
        functools.partial(_combine_kernel, final=final),
        grid_spec=pltpu.PrefetchScalarGridSpec(
            num_scalar_prefetch=3,
            grid=(t // tm,),
            in_specs=[pl.BlockSpec(memory_space=pl.ANY),
                      pl.BlockSpec((tm, LANES), lambda i, *_: (i, 0)),
                      pl.BlockSpec((tm, d), lambda i, *_: (i, 0)),
                      pl.BlockSpec((1, 1, d), lambda i, *_: (i // tps, 0, 0)),
                      pl.BlockSpec((1, d), lambda i, *_: (0, 0))],
            out_specs=pl.BlockSpec((tm, d), lambda i, *_: (i, 0)),
            scratch_shapes=[pltpu.VMEM((2, TOP_K, tm, d), F32), pltpu.SemaphoreType.DMA((2,))]),
        out_shape=jax.ShapeDtypeStruct((t, d), F32),
        compiler_params=_cparams(("arbitrary",)),
        name="moe_combine",
    )(eid, rank, offsets, yr, gate, x2, g2, final_gain.reshape(1, d))


def kernel(x, c, positions, w_ada, b_ada, norm_gain, w_in, w_out, mu_prev, mu_next, decay_w0, decay_up, iclr_a0, iclr_up, k_k, k_a, r_k, gate_up, gn_gain, gn_bias, lam_q1, lam_k1, lam_q2, lam_k2, diff_sub_gain, w_router, router_bias, w_gate, w_up, w_down, final_gain):
    batch, seq, d = x.shape
    depth = w_ada.shape[0]
    t = batch * seq
    w = RWKV_WIDTH
    x2 = x.reshape(t, d)
    mod = _ada_mod(c, w_ada, b_ada)
    rope_tabs = _rope_tables(positions)
    rope_tabs_t = _rope_tables_t(positions)
    n_rows = t * TOP_K + N_EXPERTS * MOE_BM
    for l in range(depth):
        sh1, sc1, g1, sh2, sc2, g2 = [mod[l, :, i * d:(i + 1) * d].reshape(batch, 1, d) for i in range(N_MOD)]
        w_l = w_in[l].astype(BF16)
        dw = DIFF_WIDTH
        qt, kq, vt = _project_qkv(x2, norm_gain[l, 0], sc1, sh1, w_l[:, :dw].T, w_l[:, dw:2 * dw],
                                  w_l[:, 2 * dw:3 * dw].T, seq, rope_tabs, rope_tabs_t)
        p = _project(x2, norm_gain[l, 0], sc1, sh1, w_l[:, DIFF_COLS:], seq, F32, 1152)
        lam_init = 0.8 - 0.6 * math.exp(-0.3 * l)
        lam = (jnp.exp(jnp.sum(lam_q1[l].astype(F32) * lam_k1[l].astype(F32)))
               - jnp.exp(jnp.sum(lam_q2[l].astype(F32) * lam_k2[l].astype(F32))) + lam_init)
        y_diff = _diff_attention(qt, kq, vt, lam, diff_sub_gain[l], lam_init, batch, seq)
        r, k, v, kk, lw0, lw1, a0, a1, g = _rwkv_prep(p, mu_prev[l], mu_next[l], decay_w0[l], decay_up[l],
                                                      iclr_a0[l], iclr_up[l], k_k[l], gate_up[l], seq)
        yf, yb = _rwkv_scan(r, k, v, kk, lw0, lw1, a0, a1, k_a[l], batch, seq)
        y_rwkv = _rwkv_post(yf, yb, r, k, v, a0, a1, g, k_a[l], r_k[l], gn_gain[l], gn_bias[l])
        x2, hp, eid, rank, gate, cnt = _out_proj(y_diff, y_rwkv, w_out[l].astype(BF16), x2, g1, norm_gain[l, 1],
                                                 sc2, sh2, w_router, router_bias, seq)
        offsets, block_e, n_used = _expert_layout(cnt[:, 0], n_rows // MOE_BM)
        eid = eid.reshape(TOP_K * t)
        rank = rank.reshape(TOP_K * t)
        xr = _dispatch(hp, eid, rank, offsets, n_rows)
        yr = _expert_ffn(xr, block_e, n_used, w_gate[l].astype(BF16), w_up[l].astype(BF16), w_down[l].astype(BF16))
        x2 = _combine(yr, eid, rank, offsets, gate, x2, g2, final_gain, seq, final=(l == depth - 1))
    return x2.reshape(batch, seq, d)
```

```python
import functools
import math

import jax
import jax.numpy as jnp
from jax import lax
from jax.experimental import pallas as pl
from jax.experimental.pallas import tpu as pltpu

F32 = jnp.float32
BF16 = jnp.bfloat16
U32 = jnp.uint32
HIGHEST = lax.Precision.HIGHEST

D_MODEL = 2048
DIFF_WIDTH = 1024
RWKV_WIDTH = 1024
DIFF_QK_DIM = 64
DIFF_V_DIM = 128
DIFF_HEADS = 8
ROPE_DIM = 16
ROPE_THETA = 500000.0
DIFF_NORM_EPS = 1e-5
RWKV_HEAD = 64
DECAY_LORA = 64
ICLR_LORA = 64
GATE_LORA = 128
DECAY_SCALE = 0.606531
GN_EPS = 64e-5
L2_EPS = 1e-12
N_EXPERTS = 16
N_GROUPS = 4
EXPERTS_PER_GROUP = 4
TOP_K = 2
D_FF = 1024
RMS_EPS = 1e-6
N_MOD = 6
DIFF_COLS = 3 * DIFF_WIDTH
RWKV_COLS = 3 * RWKV_WIDTH + 2 * DECAY_LORA + 2 * ICLR_LORA + GATE_LORA
LORA_COLS = RWKV_COLS - 3 * RWKV_WIDTH

LANES = 128
SCAN_CHUNK = 64
SCAN_GROUP = 256
MOE_BM = 256
VMEM_LIMIT = 56 * 1024 * 1024


def _cparams(sem):
    return pltpu.CompilerParams(dimension_semantics=sem, vmem_limit_bytes=VMEM_LIMIT)


def _tile(n, pref):
    t = min(n, pref)
    assert n % t == 0, (n, t)
    return t


def _mod_kernel(c_ref, w_ref, b_ref, o_ref):
    c = c_ref[...]
    ca = (c * jax.nn.sigmoid(c)).astype(BF16)
    o_ref[0] = jnp.dot(ca, w_ref[0].astype(BF16), preferred_element_type=F32) + b_ref[0]


def _ada_mod(c, w_ada, b_ada):
    depth, d, n = w_ada.shape
    b = c.shape[0]
    cp = jnp.zeros((8, d), F32).at[:b].set(c)
    tn = 1024
    out = pl.pallas_call(
        _mod_kernel,
        grid=(depth, n // tn),
        in_specs=[pl.BlockSpec((8, d), lambda l, j: (0, 0)),
                  pl.BlockSpec((1, d, tn), lambda l, j: (l, 0, j)),
                  pl.BlockSpec((1, 1, tn), lambda l, j: (l, 0, j))],
        out_specs=pl.BlockSpec((1, 8, tn), lambda l, j: (l, 0, j)),
        out_shape=jax.ShapeDtypeStruct((depth, 8, n), F32),
        compiler_params=_cparams(("arbitrary", "arbitrary")),
        name="ada_mod",
    )(cp, w_ada, b_ada.reshape(depth, 1, n))
    return out[:, :b]


def _rope_kernel(pos_ref, invf_ref, m1_ref, m2_ref, c_ref, s1_ref, s2_ref):
    ang = pos_ref[...] * invf_ref[...]
    s = jnp.sin(ang)
    c_ref[...] = jnp.cos(ang)
    s1_ref[...] = -s * m1_ref[...]
    s2_ref[...] = s * m2_ref[...]


def _rope_tables(positions):
    t = positions.size
    half = ROPE_DIM // 2
    inv_freq = 1.0 / (ROPE_THETA ** (jnp.arange(0, ROPE_DIM, 2, dtype=F32) / ROPE_DIM))
    lane = jnp.arange(LANES) % DIFF_QK_DIM
    m1 = (lane < half).astype(F32)
    m2 = ((lane >= half) & (lane < ROPE_DIM)).astype(F32)
    invf = jnp.where(lane < ROPE_DIM, inv_freq[lane % half], 0.0).astype(F32)
    pos = jnp.broadcast_to(positions.astype(F32).reshape(t, 1), (t, LANES))
    tm = _tile(t, 512)
    row = pl.BlockSpec((1, LANES), lambda i: (0, 0))
    blk = pl.BlockSpec((tm, LANES), lambda i: (i, 0))
    return pl.pallas_call(
        _rope_kernel,
        grid=(t // tm,),
        in_specs=[blk, row, row, row],
        out_specs=[blk, blk, blk],
        out_shape=[jax.ShapeDtypeStruct((t, LANES), F32)] * 3,
        compiler_params=_cparams(("arbitrary",)),
        name="rope_tables",
    )(pos, invf.reshape(1, LANES), m1.reshape(1, LANES), m2.reshape(1, LANES))


def _norm_mod(x, gain, sc, sh):
    ms = jnp.mean(x * x, axis=-1, keepdims=True)
    return (x * lax.rsqrt(ms + RMS_EPS) * gain) * (1.0 + sc) + sh


def _proj_kernel(x_ref, gain_ref, sc_ref, sh_ref, w_ref, o_ref, h_ref):
    @pl.when(pl.program_id(1) == 0)
    def _():
        h_ref[...] = _norm_mod(x_ref[...], gain_ref[...], sc_ref[0], sh_ref[0]).astype(BF16)

    o_ref[...] = jnp.dot(h_ref[...], w_ref[...], preferred_element_type=F32).astype(o_ref.dtype)


def _project(x2, gain, sc, sh, w, seq, out_dtype, tn):
    t, d = x2.shape
    n = w.shape[1]
    tm = _tile(seq, 512)
    tps = seq // tm
    return pl.pallas_call(
        _proj_kernel,
        grid=(t // tm, n // tn),
        in_specs=[pl.BlockSpec((tm, d), lambda i, j: (i, 0)),
                  pl.BlockSpec((1, d), lambda i, j: (0, 0)),
                  pl.BlockSpec((1, 1, d), lambda i, j: (i // tps, 0, 0)),
                  pl.BlockSpec((1, 1, d), lambda i, j: (i // tps, 0, 0)),
                  pl.BlockSpec((d, tn), lambda i, j: (0, j))],
        out_specs=pl.BlockSpec((tm, tn), lambda i, j: (i, j)),
        out_shape=jax.ShapeDtypeStruct((t, n), out_dtype),
        scratch_shapes=[pltpu.VMEM((tm, d), BF16)],
        compiler_params=_cparams(("arbitrary", "arbitrary")),
        name="proj_rwkv",
    )(x2, gain.reshape(1, d), sc, sh, w)


def _rope_t_kernel(pos_ref, invf_ref, c_ref, s_ref):
    ang = invf_ref[...] * pos_ref[...]
    c_ref[...] = jnp.cos(ang)
    s_ref[...] = jnp.sin(ang)


def _rope_tables_t(positions):
    t = positions.size
    half = ROPE_DIM // 2
    inv_freq = (1.0 / (ROPE_THETA ** (jnp.arange(0, ROPE_DIM, 2, dtype=F32) / ROPE_DIM))).reshape(half, 1)
    tm = _tile(t, 2048)
    blk = pl.BlockSpec((half, tm), lambda i: (0, i))
    return pl.pallas_call(
        _rope_t_kernel,
        grid=(t // tm,),
        in_specs=[pl.BlockSpec((1, tm), lambda i: (0, i)), pl.BlockSpec((half, 1), lambda i: (0, 0))],
        out_specs=[blk, blk],
        out_shape=[jax.ShapeDtypeStruct((half, t), F32)] * 2,
        compiler_params=_cparams(("arbitrary",)),
        name="rope_tables_t",
    )(positions.astype(F32).reshape(1, t), inv_freq)


def _qkv_kernel(x_ref, gain_ref, sc_ref, sh_ref, wqt_ref, wk_ref, wvt_ref, c_ref, s1_ref, s2_ref, ct_ref, st_ref,
                qt_o, k_o, vt_o):
    h = _norm_mod(x_ref[...], gain_ref[...], sc_ref[0], sh_ref[0]).astype(BF16)
    nt = (((1,), (1,)), ((), ()))
    half = ROPE_DIM // 2
    yq = lax.dot_general(wqt_ref[...], h, nt, preferred_element_type=F32) * (DIFF_QK_DIM ** -0.5 * math.log2(math.e))
    ct = ct_ref[...]
    st = st_ref[...]
    pieces = []
    for b0 in range(0, yq.shape[0], DIFF_QK_DIM):
        r1 = yq[b0:b0 + half]
        r2 = yq[b0 + half:b0 + ROPE_DIM]
        pieces += [r1 * ct - r2 * st, r2 * ct + r1 * st, yq[b0 + ROPE_DIM:b0 + DIFF_QK_DIM]]
    qt_o[...] = jnp.concatenate(pieces, axis=0).astype(qt_o.dtype)
    vt_o[...] = lax.dot_general(wvt_ref[...], h, nt, preferred_element_type=F32).astype(vt_o.dtype)
    yk = jnp.dot(h, wk_ref[...], preferred_element_type=F32)
    c = c_ref[...]
    s1 = s1_ref[...]
    s2 = s2_ref[...]
    for s in range(yk.shape[1] // LANES):
        ys = yk[:, s * LANES:(s + 1) * LANES]
        o = ys * c + pltpu.roll(ys, LANES - half, 1) * s1 + pltpu.roll(ys, half, 1) * s2
        k_o[:, s * LANES:(s + 1) * LANES] = o.astype(k_o.dtype)


def _project_qkv(x2, gain, sc, sh, wq_t, wk, wv_t, seq, rope_tabs, rope_tabs_t):
    t, d = x2.shape
    n = DIFF_WIDTH
    tm = _tile(seq, 512)
    tps = seq // tm
    half = ROPE_DIM // 2
    whole = lambda shape: pl.BlockSpec(shape, lambda i: (0, 0))
    mod = pl.BlockSpec((1, 1, d), lambda i: (i // tps, 0, 0))
    tab = pl.BlockSpec((tm, LANES), lambda i: (i, 0))
    tab_t = pl.BlockSpec((half, tm), lambda i: (0, i))
    return pl.pallas_call(
        _qkv_kernel,
        grid=(t // tm,),
        in_specs=[pl.BlockSpec((tm, d), lambda i: (i, 0)), whole((1, d)), mod, mod,
                  whole((n, d)), whole((d, n)), whole((n, d)), tab, tab, tab, tab_t, tab_t],
        out_specs=[pl.BlockSpec((n, tm), lambda i: (0, i)),
                   pl.BlockSpec((tm, n), lambda i: (i, 0)),
                   pl.BlockSpec((n, tm), lambda i: (0, i))],
        out_shape=[jax.ShapeDtypeStruct((n, t), BF16), jax.ShapeDtypeStruct((t, n), BF16),
                   jax.ShapeDtypeStruct((n, t), BF16)],
        compiler_params=_cparams(("arbitrary",)),
        name="proj_qkv",
    )(x2, gain.reshape(1, d), sc, sh, wq_t, wk, wv_t, *rope_tabs, *rope_tabs_t)


def _attn_kernel(qt_ref, k_ref, vt_ref, lam_ref, gain_ref, o_ref, *, tk, out_scale):
    tq = qt_ref.shape[1]
    seq = k_ref.shape[0]
    cols = 2 * tq
    q = qt_ref[...]
    row = lax.broadcasted_iota(jnp.int32, q.shape, 0)
    zero = jnp.zeros_like(q)
    qbd = jnp.concatenate([jnp.where(row < DIFF_QK_DIM, q, zero),
                           jnp.where(row >= DIFF_QK_DIM, q, zero)], axis=1)

    def scores(c):
        return jnp.dot(k_ref[c * tk:(c + 1) * tk, :], qbd, preferred_element_type=F32)

    nkv = seq // tk
    m = jnp.full((1, cols), -1e30, F32)
    l = jnp.zeros((1, cols), F32)
    acc = jnp.zeros((DIFF_V_DIM, cols), F32)
    s_next = scores(0)
    for c in range(nkv):
        s = s_next
        if c + 1 < nkv:
            s_next = scores(c + 1)
        m_new = jnp.maximum(m, jnp.max(s, axis=0, keepdims=True))
        alpha = jnp.exp2(m - m_new)
        p = jnp.exp2(s - m_new)
        l = alpha * l + jnp.sum(p, axis=0, keepdims=True)
        acc = alpha * acc + jnp.dot(vt_ref[:, c * tk:(c + 1) * tk], p.astype(BF16), preferred_element_type=F32)
        m = m_new
    o = acc / l
    d = o[:, :tq] - lam_ref[...] * o[:, tq:]
    ms = jnp.mean(d * d, axis=0, keepdims=True)
    y = (d * lax.rsqrt(ms + DIFF_NORM_EPS) * gain_ref[...]) * out_scale
    o_ref[...] = jnp.transpose(y).astype(o_ref.dtype)


def _diff_attention(qt, k, vt, lam, sub_gain, lam_init, batch, seq):
    t = k.shape[0]
    tq = _tile(seq, 256)
    tk = _tile(seq, 1024)
    nq = seq // tq
    dv = DIFF_V_DIM
    return pl.pallas_call(
        functools.partial(_attn_kernel, tk=tk, out_scale=1.0 - lam_init),
        grid=(batch, DIFF_HEADS, nq),
        in_specs=[pl.BlockSpec((dv, tq), lambda b, hh, qi: (hh, b * nq + qi)),
                  pl.BlockSpec((seq, dv), lambda b, hh, qi: (b, hh)),
                  pl.BlockSpec((dv, seq), lambda b, hh, qi: (hh, b)),
                  pl.BlockSpec((dv, 1), lambda b, hh, qi: (0, 0)),
                  pl.BlockSpec((dv, 1), lambda b, hh, qi: (0, 0))],
        out_specs=pl.BlockSpec((tq, dv), lambda b, hh, qi: (b * nq + qi, hh)),
        out_shape=jax.ShapeDtypeStruct((t, DIFF_WIDTH), BF16),
        compiler_params=_cparams(("arbitrary", "arbitrary", "arbitrary")),
        name="diff_attn",
    )(qt, k, vt, jnp.broadcast_to(lam.astype(F32).reshape(1, 1), (dv, 1)),
      sub_gain.astype(F32).reshape(dv, 1))


def _head_ones(n):
    r = lax.broadcasted_iota(jnp.int32, (n, n), 0) >> 6
    c = lax.broadcasted_iota(jnp.int32, (n, n), 1) >> 6
    return jnp.where(r == c, 1.0, 0.0).astype(BF16)


def _head_sum(x):
    ones = _head_ones(LANES)
    outs = []
    for s in range(x.shape[1] // LANES):
        xs = x[:, s * LANES:(s + 1) * LANES]
        hi = xs.astype(BF16)
        lo = (xs - hi.astype(F32)).astype(BF16)
        outs.append(jnp.dot(hi, ones, preferred_element_type=F32) + jnp.dot(lo, ones, preferred_element_type=F32))
    return jnp.concatenate(outs, axis=1)


def _rwkv_prep_kernel(p_ref, pp_ref, pn_ref, mup_ref, mun_ref, wdec_ref, w0_ref, wa_ref, a0_ref, gup_ref, kkw_ref,
                      r_o, k_o, v_o, kk_o, lw0_o, lw1_o, a0_o, a1_o, g_o, *, tiles_per_seq):
    i = pl.program_id(0)
    tm = p_ref.shape[0]
    w = RWKV_WIDTH
    first = (i % tiles_per_seq) == 0
    last = (i % tiles_per_seq) == tiles_per_seq - 1

    def shifted(c0, c1):
        p = p_ref[:, c0:c1]
        prow = jnp.where(first, 0.0, pp_ref[7:8, c0:c1])
        nrow = jnp.where(last, 0.0, pn_ref[0:1, c0:c1])
        row = lax.broadcasted_iota(jnp.int32, p.shape, 0)
        prev = jnp.where(row == 0, prow, pltpu.roll(p, 1, 0))
        nxt = jnp.where(row == tm - 1, nrow, pltpu.roll(p, tm - 1, 0))
        return p + mup_ref[:, c0:c1] * (prev - p) + mun_ref[:, c0:c1] * (nxt - p)

    lo = shifted(3 * w, 3 * w + LORA_COLS)
    wd = jnp.tanh(lo[:, :2 * DECAY_LORA]).astype(BF16)
    ad = lo[:, 2 * DECAY_LORA:2 * DECAY_LORA + 2 * ICLR_LORA].astype(BF16)
    gd = jax.nn.sigmoid(lo[:, 2 * DECAY_LORA + 2 * ICLR_LORA:]).astype(BF16)
    lw = -DECAY_SCALE * jax.nn.sigmoid(w0_ref[...] + jnp.dot(wd, wdec_ref[...], preferred_element_type=F32))
    lw0_o[...] = lw[:, :w]
    lw1_o[...] = lw[:, w:]
    a = jax.nn.sigmoid(a0_ref[...] + jnp.dot(ad, wa_ref[...], preferred_element_type=F32))
    a0_o[...] = a[:, :w].astype(a0_o.dtype)
    a1_o[...] = a[:, w:].astype(a1_o.dtype)
    g_o[...] = jnp.dot(gd, gup_ref[...], preferred_element_type=F32).astype(g_o.dtype)
    r_o[...] = shifted(0, w).astype(r_o.dtype)
    k = shifted(w, 2 * w)
    k_o[...] = k.astype(k_o.dtype)
    v_o[...] = shifted(2 * w, 3 * w).astype(v_o.dtype)
    kk = k * kkw_ref[...]
    nrm = jnp.sqrt(_head_sum(kk * kk))
    kk_o[...] = (kk / jnp.maximum(nrm, L2_EPS)).astype(kk_o.dtype)


def _rwkv_prep(p, mu_prev, mu_next, decay_w0, decay_up, iclr_a0, iclr_up, k_k, gate_up, seq):
    t = p.shape[0]
    w = RWKV_WIDTH
    tm = _tile(seq, 256)
    tps = seq // tm
    nblk8 = t // 8
    z = jnp.zeros((DECAY_LORA, w), F32)
    wdec = jnp.concatenate([jnp.concatenate([decay_up[0], z], 0), jnp.concatenate([z, decay_up[1]], 0)], 1).astype(BF16)
    wa = jnp.concatenate([jnp.concatenate([iclr_up[0], z], 0), jnp.concatenate([z, iclr_up[1]], 0)], 1).astype(BF16)
    full = lambda shape: pl.BlockSpec(shape, lambda i: (0,) * len(shape))
    blk = pl.BlockSpec((tm, w), lambda i: (i, 0))
    outs = pl.pallas_call(
        functools.partial(_rwkv_prep_kernel, tiles_per_seq=tps),
        grid=(t // tm,),
        in_specs=[pl.BlockSpec((tm, RWKV_COLS), lambda i: (i, 0)),
                  pl.BlockSpec((8, RWKV_COLS), lambda i: (jnp.maximum(i * (tm // 8) - 1, 0), 0)),
                  pl.BlockSpec((8, RWKV_COLS), lambda i: (jnp.minimum((i + 1) * (tm // 8), nblk8 - 1), 0)),
                  full((1, RWKV_COLS)), full((1, RWKV_COLS)),
                  full((2 * DECAY_LORA, 2 * w)), full((1, 2 * w)),
                  full((2 * ICLR_LORA, 2 * w)), full((1, 2 * w)),
                  full((GATE_LORA, w)), full((1, w))],
        out_specs=[blk] * 9,
        out_shape=([jax.ShapeDtypeStruct((t, w), BF16)] * 4 + [jax.ShapeDtypeStruct((t, w), F32)] * 2
                   + [jax.ShapeDtypeStruct((t, w), BF16)] * 3),
        compiler_params=_cparams(("arbitrary",)),
        name="rwkv_prep",
    )(p, p, p, mu_prev.reshape(1, -1), mu_next.reshape(1, -1), wdec, decay_w0.reshape(1, 2 * w),
      wa, iclr_a0.reshape(1, 2 * w), gate_up.astype(BF16), k_k.reshape(1, w))
    return outs


def _split2(x):
    hi = x.astype(BF16)
    return hi, (x - hi.astype(F32)).astype(BF16)


def _scan_chunks(chains):
    c, g = chains[0][0].shape
    nh = g // RWKV_HEAD
    pc = nh * c
    lp = dict(preferred_element_type=F32)
    nt = (((1,), (1,)), ((), ()))
    tn = (((0,), (0,)), ((), ()))
    fwds = [ch[7] for ch in chains]
    n = len(chains)
    each = lambda f, *ls: [f(*xs) for xs in zip(*ls)]

    ti = lax.broadcasted_iota(jnp.int32, (c, c), 0)
    si = lax.broadcasted_iota(jnp.int32, (c, c), 1)
    tp = lax.broadcasted_iota(jnp.int32, (c, pc), 0)
    sp = lax.broadcasted_iota(jnp.int32, (c, pc), 1) & (c - 1)
    tri = {f: jnp.where((si <= ti) if f else (si >= ti), 1.0, 0.0).astype(BF16) for f in set(fwds)}
    strict = {f: (sp < tp) if f else (sp > tp) for f in set(fwds)}
    incl = {f: (sp <= tp) if f else (sp >= tp) for f in set(fwds)}
    eye = jnp.where(sp == tp, 1.0, 0.0).astype(F32)
    lane_head = lax.broadcasted_iota(jnp.int32, (c, g), 1) >> 6
    rh = lax.broadcasted_iota(jnp.int32, (g, g), 0) >> 6
    chh = lax.broadcasted_iota(jnp.int32, (g, g), 1) >> 6

    def bd(x):
        z = jnp.zeros_like(x)
        return jnp.concatenate([jnp.where(lane_head == h, x, z) for h in range(nh)], axis=0)

    def mm(x, y):
        return jnp.dot(x.astype(BF16), bd(y.astype(BF16)), **lp)

    r, k, v, kk, lw, a = [[ch[i] for ch in chains] for i in range(6)]
    st_refs = [ch[6] for ch in chains]

    def cum(f, lw_):
        hi, lo = _split2(lw_)
        return jnp.dot(tri[f], hi, **lp) + jnp.dot(tri[f], lo, **lp)

    lc = each(cum, fwds, lw)
    ltot = each(lambda f, x: x[c - 1:c, :] if f else x[0:1, :], fwds, lc)
    e_inc = each(jnp.exp, lc)
    e_inv = each(lambda x: jnp.exp(-x), lc)
    e_rem = each(lambda lt, x: jnp.exp(lt - x), ltot, lc)
    kd = each(lambda k_, a_, ka_: k_ * (1.0 + (a_ - 1.0) * ka_), k, a, [ch[8] for ch in chains])
    b = each(lambda kk_, a_: kk_ * a_, kk, a)
    a_t = each(lambda kk_, lc_, lw_: (-kk_ * jnp.exp(lc_ - lw_)).astype(BF16), kk, lc, lw)
    r_t = each(lambda r_, e: (r_ * e).astype(BF16), r, e_inc)
    b_t = each(lambda b_, e: (b_ * e).astype(BF16), b, e_inv)
    k_t = each(lambda kd_, e: (kd_ * e).astype(BF16), kd, e_inv)
    b_h = each(lambda b_, e: (b_ * e).astype(BF16), b, e_rem)
    k_h = each(lambda kd_, e: (kd_ * e).astype(BF16), kd, e_rem)
    vb = each(lambda v_: v_.astype(BF16), v)

    ar = each(lambda x, y: jnp.concatenate([x, y], axis=0), a_t, r_t)
    m = each(lambda ar_, b_, k_: lax.dot_general(ar_, jnp.concatenate([bd(b_), bd(k_)], axis=0), nt, **lp),
             ar, b_t, k_t)
    n1 = each(lambda f, m_: jnp.where(strict[f], m_[:c, :pc], 0.0), fwds, m)
    m_ak = each(lambda f, m_: jnp.where(strict[f], m_[:c, pc:], 0.0).astype(BF16), fwds, m)
    m_r = each(lambda f, m_: jnp.concatenate([jnp.where(incl[f], m_[c:, :pc], 0.0),
                                              jnp.where(incl[f], m_[c:, pc:], 0.0)], axis=1).astype(BF16), fwds, m)

    pinv = each(lambda x: eye + x, n1)
    nk = each(mm, n1, n1)
    for _ in range(int(math.log2(c)) - 2):
        rr = each(lambda p_, nk_: mm(jnp.concatenate([p_, nk_], axis=0), nk_), pinv, nk)
        pinv = each(lambda p_, rr_: p_ + rr_[:c], pinv, rr)
        nk = [rr_[c:] for rr_ in rr]
    pinv = each(lambda p_, nk_: p_ + mm(p_, nk_), pinv, nk)

    st = [ref[...] for ref in st_refs]
    xs = each(lambda ar_, st_: lax.dot_general(ar_, st_.astype(BF16), nt, **lp), ar, st)
    x1 = each(lambda xs_, mak, vb_: xs_[:c] + jnp.dot(mak, bd(vb_), **lp), xs, m_ak, vb)
    ub = each(lambda p_, x_: mm(p_, x_).astype(BF16), pinv, x1)
    y = each(lambda xs_, mr, ub_, vb_: xs_[c:] + jnp.dot(mr, jnp.concatenate([bd(ub_), bd(vb_)], axis=0), **lp),
             xs, m_r, ub, vb)
    ds = each(lambda ub_, vb_, bh, kh: lax.dot_general(jnp.concatenate([ub_, vb_], axis=0),
                                                       jnp.concatenate([bh, kh], axis=0), tn, **lp),
              ub, vb, b_h, k_h)
    for i in range(n):
        st_refs[i][...] = st[i] * jnp.exp(ltot[i]) + jnp.where(rh == chh, ds[i], 0.0)
    return y


def _scan_kernel(rf, kf, vf, kkf, lwf, af, rb, kb, vb, kkb, lwb, ab, ka_ref, yf_o, yb_o, stf, stb):
    @pl.when(pl.program_id(1) == 0)
    def _():
        stf[...] = jnp.zeros_like(stf)
        stb[...] = jnp.zeros_like(stb)

    nb = rf.shape[0]
    g = SCAN_GROUP
    halves = rf.shape[2] // g
    chains, outs = [], []
    for hh in range(halves):
        cs = slice(hh * g, (hh + 1) * g)
        ld = lambda ref, b: ref[b, :, cs].astype(F32)
        ka = ka_ref[:, cs]
        for b in range(nb):
            chains.append((ld(rf, b), ld(kf, b), ld(vf, b), ld(kkf, b), ld(lwf, b), ld(af, b),
                           stf.at[hh * nb + b], True, ka))
            outs.append((yf_o, b, cs))
            chains.append((ld(rb, b), ld(kb, b), ld(vb, b), ld(kkb, b), ld(lwb, b), ld(ab, b),
                           stb.at[hh * nb + b], False, ka))
            outs.append((yb_o, b, cs))
    ys = _scan_chunks(chains)
    for (o_ref, b, cs), y in zip(outs, ys):
        o_ref[b, :, cs] = y


def _rwkv_scan(r, k, v, kk, lw0, lw1, a0, a1, k_a, batch, seq):
    t, w = r.shape
    c = _tile(seq, SCAN_CHUNK)
    g = SCAN_GROUP
    gw = 2 * g
    nc = seq // c
    fwd = pl.BlockSpec((batch, c, gw), lambda gi, ci: (0, ci, gi))
    bwd = pl.BlockSpec((batch, c, gw), lambda gi, ci: (0, nc - 1 - ci, gi))
    b3 = lambda z: z.reshape(batch, seq, w)
    nst = batch * (gw // g)
    yf, yb = pl.pallas_call(
        _scan_kernel,
        grid=(w // gw, nc),
        in_specs=[fwd] * 6 + [bwd] * 6 + [pl.BlockSpec((1, gw), lambda gi, ci: (0, gi))],
        out_specs=[fwd, bwd],
        out_shape=[jax.ShapeDtypeStruct((batch, seq, w), F32)] * 2,
        scratch_shapes=[pltpu.VMEM((nst, g, g), F32), pltpu.VMEM((nst, g, g), F32)],
        compiler_params=_cparams(("arbitrary", "arbitrary")),
        name="rwkv_scan",
    )(b3(r), b3(k), b3(v), b3(kk), b3(lw0), b3(a0), b3(r), b3(k), b3(v), b3(kk), b3(lw1), b3(a1), k_a.reshape(1, w))
    return yf.reshape(t, w), yb.reshape(t, w)


def _rwkv_post_kernel(yf, yb, r, k, v, a0, a1, g, ka, rk, gg, gb, o_ref):
    inv = 1.0 / RWKV_HEAD
    y = yf[...] + yb[...]
    mean = _head_sum(y) * inv
    yc = y - mean
    var = _head_sum(yc * yc) * inv
    yn = yc * lax.rsqrt(var + GN_EPS) * gg[...] + gb[...]
    f = lambda ref: ref[...].astype(F32)
    ksum = f(k) * (2.0 + (f(a0) + f(a1) - 2.0) * ka[...])
    bonus = _head_sum(f(r) * ksum * rk[...]) * f(v)
    o_ref[...] = ((yn + bonus) * f(g)).astype(o_ref.dtype)


def _rwkv_post(yf, yb, r, k, v, a0, a1, g, k_a, r_k, gn_gain, gn_bias):
    t, w = yf.shape
    tm = _tile(t, 256)
    blk = pl.BlockSpec((tm, w), lambda i: (i, 0))
    row = pl.BlockSpec((1, w), lambda i: (0, 0))
    return pl.pallas_call(
        _rwkv_post_kernel,
        grid=(t // tm,),
        in_specs=[blk] * 8 + [row] * 4,
        out_specs=blk,
        out_shape=jax.ShapeDtypeStruct((t, w), BF16),
        compiler_params=_cparams(("arbitrary",)),
        name="rwkv_post",
    )(yf, yb, r, k, v, a0, a1, g, k_a.reshape(1, w), r_k.reshape(1, w), gn_gain.reshape(1, w), gn_bias.reshape(1, w))


def _route_tile(sc, sel):
    rows = lambda m: [m[e:e + 1, :] for e in range(N_EXPERTS)]
    sel_r = rows(sel)
    sc_r = rows(sc)
    npg = EXPERTS_PER_GROUP

    def top2_sum(a, b, c, d):
        return jnp.maximum(jnp.maximum(jnp.maximum(a + b, a + c), jnp.maximum(a + d, b + c)),
                           jnp.maximum(b + d, c + d))

    best = top2_sum(*sel_r[0:npg])
    grp = jnp.zeros_like(best, dtype=jnp.int32)
    for gi in range(1, N_GROUPS):
        gs = top2_sum(*sel_r[gi * npg:(gi + 1) * npg])
        upd = gs > best
        grp = jnp.where(upd, gi, grp)
        best = jnp.where(upd, gs, best)

    def in_group(r, j):
        out = r[(N_GROUPS - 1) * npg + j]
        for gi in range(N_GROUPS - 2, -1, -1):
            out = jnp.where(grp == gi, r[gi * npg + j], out)
        return out

    v = [in_group(sel_r, j) for j in range(npg)]
    s = [in_group(sc_r, j) for j in range(npg)]
    b1, i1, s1 = v[0], jnp.zeros_like(grp), s[0]
    for j in range(1, npg):
        upd = v[j] > b1
        b1 = jnp.where(upd, v[j], b1)
        i1 = jnp.where(upd, j, i1)
        s1 = jnp.where(upd, s[j], s1)
    neg = jnp.full_like(b1, -jnp.inf)
    b2, i2, s2 = neg, jnp.zeros_like(grp), s[0]
    for j in range(npg):
        upd = jnp.where(i1 == j, neg, v[j]) > b2
        b2 = jnp.where(upd, v[j], b2)
        i2 = jnp.where(upd, j, i2)
        s2 = jnp.where(upd, s[j], s2)
    den = s1 + s2
    return grp * npg + i1, grp * npg + i2, s1 / den, s2 / den


def _outproj_kernel(yd_ref, yr_ref, wo_ref, x_ref, g1_ref, gain_ref, sc_ref, sh_ref, wrt_ref, bias_ref,
                    x_o, hp_o, eid_o, rank_o, gate_o, cnt_o, run_ref):
    @pl.when(pl.program_id(0) == 0)
    def _():
        run_ref[...] = jnp.zeros_like(run_ref)

    tm = x_ref.shape[0]
    half = DIFF_WIDTH
    mix = (jnp.dot(yd_ref[...], wo_ref[:half, :], preferred_element_type=F32)
           + jnp.dot(yr_ref[...], wo_ref[half:, :], preferred_element_type=F32))
    xn = x_ref[...] + g1_ref[0] * mix
    x_o[...] = xn
    hb = _norm_mod(xn, gain_ref[...], sc_ref[0], sh_ref[0]).astype(BF16)
    u = pltpu.bitcast(hb.astype(F32), U32)
    hd = D_MODEL // 2
    hp_o[...] = (u[:, :hd] & jnp.uint32(0xFFFF0000)) | (u[:, hd:] >> 16)

    logits = lax.dot_general(wrt_ref[...], hb, (((1,), (1,)), ((), ())), preferred_element_type=F32)
    sc = jax.nn.sigmoid(logits)
    e1, e2, g1, g2 = _route_tile(sc, sc + bias_ref[...])
    eid_o[0:1, :] = e1
    eid_o[1:2, :] = e2
    eio = lax.broadcasted_iota(jnp.int32, (N_EXPERTS, tm), 0)
    oh1 = eio == e1
    oh2 = eio == e2
    oh = jnp.where(oh1, 1.0, 0.0) + jnp.where(oh2, 1.0, 0.0)
    before = (lax.broadcasted_iota(jnp.int32, (tm, tm), 0) < lax.broadcasted_iota(jnp.int32, (tm, tm), 1))
    base = (jnp.dot(oh.astype(BF16), jnp.where(before, 1.0, 0.0).astype(BF16), preferred_element_type=F32)
            + run_ref[:, 0:1])
    rank_o[0:1, :] = jnp.sum(jnp.where(oh1, base, 0.0), axis=0, keepdims=True).astype(jnp.int32)
    rank_o[1:2, :] = jnp.sum(jnp.where(oh2, base, 0.0), axis=0, keepdims=True).astype(jnp.int32)
    run = run_ref[...] + jnp.sum(oh, axis=1, keepdims=True)
    run_ref[...] = run
    cnt_o[...] = run
    ri = lax.broadcasted_iota(jnp.int32, (LANES, tm), 0)
    gate_o[...] = jnp.transpose(jnp.where(ri == 0, g1, jnp.where(ri == 1, g2, 0.0)))


def _out_proj(yd, yr, w_out, x2, g1, gain, sc, sh, w_router, router_bias, seq):
    t, d = x2.shape
    tm = _tile(seq, 256)
    tps = seq // tm
    mod = pl.BlockSpec((1, 1, d), lambda i: (i // tps, 0, 0))
    ne = N_EXPERTS
    return pl.pallas_call(
        _outproj_kernel,
        grid=(t // tm,),
        in_specs=[pl.BlockSpec((tm, DIFF_WIDTH), lambda i: (i, 0)),
                  pl.BlockSpec((tm, RWKV_WIDTH), lambda i: (i, 0)),
                  pl.BlockSpec((d, d), lambda i: (0, 0)),
                  pl.BlockSpec((tm, d), lambda i: (i, 0)),
                  mod,
                  pl.BlockSpec((1, d), lambda i: (0, 0)),
                  mod, mod,
                  pl.BlockSpec((ne, d), lambda i: (0, 0)),
                  pl.BlockSpec((ne, 1), lambda i: (0, 0))],
        out_specs=[pl.BlockSpec((tm, d), lambda i: (i, 0)),
                   pl.BlockSpec((tm, d // 2), lambda i: (i, 0)),
                   pl.BlockSpec((TOP_K, tm), lambda i: (0, i)),
                   pl.BlockSpec((TOP_K, tm), lambda i: (0, i)),
                   pl.BlockSpec((tm, LANES), lambda i: (i, 0)),
                   pl.BlockSpec((ne, LANES), lambda i: (0, 0))],
        out_shape=[jax.ShapeDtypeStruct((t, d), F32),
                   jax.ShapeDtypeStruct((t, d // 2), U32),
                   jax.ShapeDtypeStruct((TOP_K, t), jnp.int32),
                   jax.ShapeDtypeStruct((TOP_K, t), jnp.int32),
                   jax.ShapeDtypeStruct((t, LANES), F32),
                   jax.ShapeDtypeStruct((ne, LANES), F32)],
        scratch_shapes=[pltpu.VMEM((ne, LANES), F32)],
        compiler_params=_cparams(("arbitrary",)),
        name="out_proj",
    )(yd, yr, w_out, x2, g1, gain.reshape(1, d), sc, sh, w_router.T.astype(BF16),
      router_bias.astype(F32).reshape(ne, 1))


def _expert_layout(counts, n_blocks):
    counts = counts.astype(jnp.int32)
    padded = ((counts + MOE_BM - 1) // MOE_BM) * MOE_BM
    padded_ends = jnp.cumsum(padded)
    offsets = (padded_ends - padded).astype(jnp.int32)
    block_start = jnp.arange(n_blocks, dtype=jnp.int32) * MOE_BM
    block_e = jnp.minimum(jnp.sum((block_start[:, None] >= padded_ends[None, :]).astype(jnp.int32), axis=1),
                          N_EXPERTS - 1).astype(jnp.int32)
    n_used = (padded_ends[-1] // MOE_BM).astype(jnp.int32).reshape(1)
    return offsets, block_e, n_used


def _dispatch_kernel(dest_ref, h_ref, xin_ref, xr_ref, hbuf, sem_in, sem_out, *, tm):
    del xin_ref
    i = pl.program_id(0)
    n = pl.num_programs(0)
    t = dest_ref.shape[0] // TOP_K
    slot = i % 2

    def load(tile, s):
        return pltpu.make_async_copy(h_ref.at[pl.ds(tile * tm, tm), :], hbuf.at[s], sem_in.at[s])

    def wait_out(s):
        for kx in range(TOP_K):
            pltpu.make_async_copy(hbuf.at[s], xr_ref.at[pl.ds(0, tm), :], sem_out.at[s]).wait()

    @pl.when(i == 0)
    def _():
        load(0, 0).start()

    load(i, slot).wait()

    def issue(r, carry):
        for kx in range(TOP_K):
            d = dest_ref[kx * t + i * tm + r]
            pltpu.make_async_copy(hbuf.at[slot, pl.ds(r, 1), :], xr_ref.at[pl.ds(d, 1), :], sem_out.at[slot]).start()
        return carry

    lax.fori_loop(0, tm, issue, 0, unroll=8)

    @pl.when(i > 0)
    def _():
        wait_out(1 - slot)

    @pl.when(i + 1 < n)
    def _():
        load(i + 1, 1 - slot).start()

    @pl.when(i == n - 1)
    def _():
        wait_out(slot)


def _dispatch(hp, dest, n_rows):
    t, hd = hp.shape
    tm = _tile(t, 256)
    zeros = jnp.zeros((n_rows, hd), U32)
    return pl.pallas_call(
        functools.partial(_dispatch_kernel, tm=tm),
        grid_spec=pltpu.PrefetchScalarGridSpec(
            num_scalar_prefetch=1,
            grid=(t // tm,),
            in_specs=[pl.BlockSpec(memory_space=pl.ANY),
                      pl.BlockSpec(memory_space=pl.ANY)],
            out_specs=pl.BlockSpec(memory_space=pl.ANY),
            scratch_shapes=[pltpu.VMEM((2, tm, hd), U32), pltpu.SemaphoreType.DMA((2,)),
                            pltpu.SemaphoreType.DMA((2,))]),
        out_shape=jax.ShapeDtypeStruct((n_rows, hd), U32),
        input_output_aliases={2: 0},
        compiler_params=_cparams(("arbitrary",)),
        name="moe_dispatch",
    )(dest, hp, zeros)


def _ffn_kernel(be_ref, nu_ref, x_ref, wg_ref, wu_ref, wd_ref, o_ref):
    del be_ref
    i = pl.program_id(0)

    @pl.when(i < nu_ref[0])
    def _():
        xp = x_ref[...]
        xa = pltpu.bitcast(xp & jnp.uint32(0xFFFF0000), F32).astype(BF16)
        xb = pltpu.bitcast(xp << 16, F32).astype(BF16)
        x = jnp.concatenate([xa, xb], axis=1)
        hg = jnp.dot(x, wg_ref[0], preferred_element_type=F32)
        hu = jnp.dot(x, wu_ref[0], preferred_element_type=F32)
        h = (hg * jax.nn.sigmoid(hg) * hu).astype(BF16)
        o_ref[...] = jnp.dot(h, wd_ref[0], preferred_element_type=F32)

    @pl.when(i >= nu_ref[0])
    def _():
        o_ref[...] = jnp.zeros_like(o_ref)


def _expert_ffn(xr, block_e, n_used, wg, wu, wd):
    p, hd = xr.shape
    d = 2 * hd
    nb = p // MOE_BM
    return pl.pallas_call(
        _ffn_kernel,
        grid_spec=pltpu.PrefetchScalarGridSpec(
            num_scalar_prefetch=2,
            grid=(nb,),
            in_specs=[pl.BlockSpec((MOE_BM, hd), lambda i, be, nu: (i, 0)),
                      pl.BlockSpec((1, d, D_FF), lambda i, be, nu: (be[i], 0, 0)),
                      pl.BlockSpec((1, d, D_FF), lambda i, be, nu: (be[i], 0, 0)),
                      pl.BlockSpec((1, D_FF, d), lambda i, be, nu: (be[i], 0, 0))],
            out_specs=pl.BlockSpec((MOE_BM, d), lambda i, be, nu: (i, 0))),
        out_shape=jax.ShapeDtypeStruct((p, d), F32),
        compiler_params=_cparams(("arbitrary",)),
        name="moe_ffn",
    )(block_e, n_used, xr, wg, wu, wd)


def _combine_kernel(dest_ref, yr_ref, gate_ref, x_ref, g2_ref, fg_ref, o_ref, buf, sem, *, final):
    i = pl.program_id(0)
    n = pl.num_programs(0)
    tm = x_ref.shape[0]
    t = dest_ref.shape[0] // TOP_K

    def issue_tile(tile, slot):
        def issue(r, carry):
            for kx in range(TOP_K):
                d = dest_ref[kx * t + tile * tm + r]
                pltpu.make_async_copy(yr_ref.at[pl.ds(d, 1), :], buf.at[slot, kx, pl.ds(r, 1), :],
                                      sem.at[slot]).start()
            return carry

        lax.fori_loop(0, tm, issue, 0, unroll=8)

    @pl.when(i == 0)
    def _():
        issue_tile(0, 0)

    @pl.when(i + 1 < n)
    def _():
        issue_tile(i + 1, (i + 1) % 2)

    slot = i % 2
    for kx in range(TOP_K):
        pltpu.make_async_copy(yr_ref.at[pl.ds(0, tm), :], buf.at[slot, kx], sem.at[slot]).wait()

    gate = gate_ref[...]
    y = gate[:, 0:1] * buf[slot, 0] + gate[:, 1:2] * buf[slot, 1]
    xn = x_ref[...] + g2_ref[0] * y
    if final:
        ms = jnp.mean(xn * xn, axis=-1, keepdims=True)
        xn = xn * lax.rsqrt(ms + RMS_EPS) * fg_ref[...]
    o_ref[...] = xn


def _combine(yr, dest, gate, x2, g2, final_gain, seq, final):
    t, d = x2.shape
    tm = _tile(seq, 256)
    tps = seq // tm
    return pl.pallas_call(
        functools.partial(_combine_kernel, final=final),
        grid_spec=pltpu.PrefetchScalarGridSpec(
            num_scalar_prefetch=1,
            grid=(t // tm,),
            in_specs=[pl.BlockSpec(memory_space=pl.ANY),
                      pl.BlockSpec((tm, LANES), lambda i, *_: (i, 0)),
                      pl.BlockSpec((tm, d), lambda i, *_: (i, 0)),
                      pl.BlockSpec((1, 1, d), lambda i, *_: (i // tps, 0, 0)),
                      pl.BlockSpec((1, d), lambda i, *_: (0, 0))],
            out_specs=pl.BlockSpec((tm, d), lambda i, *_: (i, 0)),
            scratch_shapes=[pltpu.VMEM((2, TOP_K, tm, d), F32), pltpu.SemaphoreType.DMA((2,))]),
        out_shape=jax.ShapeDtypeStruct((t, d), F32),
        compiler_params=_cparams(("arbitrary",)),
        name="moe_combine",
    )(dest, yr, gate, x2, g2, final_gain.reshape(1, d))


def kernel(x, c, positions, w_ada, b_ada, norm_gain, w_in, w_out, mu_prev, mu_next, decay_w0, decay_up, iclr_a0, iclr_up, k_k, k_a, r_k, gate_up, gn_gain, gn_bias, lam_q1, lam_k1, lam_q2, lam_k2, diff_sub_gain, w_router, router_bias, w_gate, w_up, w_down, final_gain):
    batch, seq, d = x.shape
    depth = w_ada.shape[0]
    t = batch * seq
    w = RWKV_WIDTH
    x2 = x.reshape(t, d)
    mod = _ada_mod(c, w_ada, b_ada)
    rope_tabs = _rope_tables(positions)
    rope_tabs_t = _rope_tables_t(positions)
    n_rows = t * TOP_K + N_EXPERTS * MOE_BM
    for l in range(depth):
        sh1, sc1, g1, sh2, sc2, g2 = [mod[l, :, i * d:(i + 1) * d].reshape(batch, 1, d) for i in range(N_MOD)]
        w_l = w_in[l].astype(BF16)
        dw = DIFF_WIDTH
        qt, kq, vt = _project_qkv(x2, norm_gain[l, 0], sc1, sh1, w_l[:, :dw].T, w_l[:, dw:2 * dw],
                                  w_l[:, 2 * dw:3 * dw].T, seq, rope_tabs, rope_tabs_t)
        p = _project(x2, norm_gain[l, 0], sc1, sh1, w_l[:, DIFF_COLS:], seq, F32, 1152)
        lam_init = 0.8 - 0.6 * math.exp(-0.3 * l)
        lam = (jnp.exp(jnp.sum(lam_q1[l].astype(F32) * lam_k1[l].astype(F32)))
               - jnp.exp(jnp.sum(lam_q2[l].astype(F32) * lam_k2[l].astype(F32))) + lam_init)
        y_diff = _diff_attention(qt, kq, vt, lam, diff_sub_gain[l], lam_init, batch, seq)
        r, k, v, kk, lw0, lw1, a0, a1, g = _rwkv_prep(p, mu_prev[l], mu_next[l], decay_w0[l], decay_up[l],
                                                      iclr_a0[l], iclr_up[l], k_k[l], gate_up[l], seq)
        yf, yb = _rwkv_scan(r, k, v, kk, lw0, lw1, a0, a1, k_a[l], batch, seq)
        y_rwkv = _rwkv_post(yf, yb, r, k, v, a0, a1, g, k_a[l], r_k[l], gn_gain[l], gn_bias[l])
        x2, hp, eid, rank, gate, cnt = _out_proj(y_diff, y_rwkv, w_out[l].astype(BF16), x2, g1, norm_gain[l, 1],
                                                 sc2, sh2, w_router, router_bias, seq)
        offsets, block_e, n_used = _expert_layout(cnt[:, 0], n_rows // MOE_BM)
        onehot = eid[:, :, None] == jnp.arange(N_EXPERTS, dtype=jnp.int32)
        dest = (rank + jnp.sum(jnp.where(onehot, offsets, 0), axis=-1)).reshape(TOP_K * t)
        xr = _dispatch(hp, dest, n_rows)
        yr = _expert_ffn(xr, block_e, n_used, w_gate[l].astype(BF16), w_up[l].astype(BF16), w_down[l].astype(BF16))
        x2 = _combine(yr, dest, gate, x2, g2, final_gain, seq, final=(l == depth - 1))
    return x2.reshape(batch, seq, d)
```

```python
import functools
import math

import jax
import jax.numpy as jnp
from jax import lax
from jax.experimental import pallas as pl
from jax.experimental.pallas import tpu as pltpu

F32 = jnp.float32
BF16 = jnp.bfloat16
U32 = jnp.uint32
HIGHEST = lax.Precision.HIGHEST

D_MODEL = 2048
DIFF_WIDTH = 1024
RWKV_WIDTH = 1024
DIFF_QK_DIM = 64
DIFF_V_DIM = 128
DIFF_HEADS = 8
ROPE_DIM = 16
ROPE_THETA = 500000.0
DIFF_NORM_EPS = 1e-5
RWKV_HEAD = 64
DECAY_LORA = 64
ICLR_LORA = 64
GATE_LORA = 128
DECAY_SCALE = 0.606531
GN_EPS = 64e-5
L2_EPS = 1e-12
N_EXPERTS = 16
N_GROUPS = 4
EXPERTS_PER_GROUP = 4
TOP_K = 2
D_FF = 1024
RMS_EPS = 1e-6
N_MOD = 6
DIFF_COLS = 3 * DIFF_WIDTH
RWKV_COLS = 3 * RWKV_WIDTH + 2 * DECAY_LORA + 2 * ICLR_LORA + GATE_LORA
LORA_COLS = RWKV_COLS - 3 * RWKV_WIDTH

LANES = 128
SCAN_CHUNK = 64
SCAN_GROUP = 256
MOE_BM = 256
VMEM_LIMIT = 56 * 1024 * 1024


def _cparams(sem):
    return pltpu.CompilerParams(dimension_semantics=sem, vmem_limit_bytes=VMEM_LIMIT)


def _tile(n, pref):
    t = min(n, pref)
    assert n % t == 0, (n, t)
    return t


def _mod_kernel(c_ref, w_ref, b_ref, o_ref):
    c = c_ref[...]
    ca = (c * jax.nn.sigmoid(c)).astype(BF16)
    o_ref[0] = jnp.dot(ca, w_ref[0].astype(BF16), preferred_element_type=F32) + b_ref[0]


def _ada_mod(c, w_ada, b_ada):
    depth, d, n = w_ada.shape
    b = c.shape[0]
    cp = jnp.zeros((8, d), F32).at[:b].set(c)
    tn = 1024
    out = pl.pallas_call(
        _mod_kernel,
        grid=(depth, n // tn),
        in_specs=[pl.BlockSpec((8, d), lambda l, j: (0, 0)),
                  pl.BlockSpec((1, d, tn), lambda l, j: (l, 0, j)),
                  pl.BlockSpec((1, 1, tn), lambda l, j: (l, 0, j))],
        out_specs=pl.BlockSpec((1, 8, tn), lambda l, j: (l, 0, j)),
        out_shape=jax.ShapeDtypeStruct((depth, 8, n), F32),
        compiler_params=_cparams(("arbitrary", "arbitrary")),
        name="ada_mod",
    )(cp, w_ada, b_ada.reshape(depth, 1, n))
    return out[:, :b]


def _rope_kernel(pos_ref, invf_ref, m1_ref, m2_ref, c_ref, s1_ref, s2_ref):
    ang = pos_ref[...] * invf_ref[...]
    s = jnp.sin(ang)
    c_ref[...] = jnp.cos(ang)
    s1_ref[...] = -s * m1_ref[...]
    s2_ref[...] = s * m2_ref[...]


def _rope_tables(positions):
    t = positions.size
    half = ROPE_DIM // 2
    inv_freq = 1.0 / (ROPE_THETA ** (jnp.arange(0, ROPE_DIM, 2, dtype=F32) / ROPE_DIM))
    lane = jnp.arange(LANES) % DIFF_QK_DIM
    m1 = (lane < half).astype(F32)
    m2 = ((lane >= half) & (lane < ROPE_DIM)).astype(F32)
    invf = jnp.where(lane < ROPE_DIM, inv_freq[lane % half], 0.0).astype(F32)
    pos = jnp.broadcast_to(positions.astype(F32).reshape(t, 1), (t, LANES))
    tm = _tile(t, 512)
    row = pl.BlockSpec((1, LANES), lambda i: (0, 0))
    blk = pl.BlockSpec((tm, LANES), lambda i: (i, 0))
    return pl.pallas_call(
        _rope_kernel,
        grid=(t // tm,),
        in_specs=[blk, row, row, row],
        out_specs=[blk, blk, blk],
        out_shape=[jax.ShapeDtypeStruct((t, LANES), F32)] * 3,
        compiler_params=_cparams(("arbitrary",)),
        name="rope_tables",
    )(pos, invf.reshape(1, LANES), m1.reshape(1, LANES), m2.reshape(1, LANES))


def _norm_mod(x, gain, sc, sh):
    ms = jnp.mean(x * x, axis=-1, keepdims=True)
    return (x * lax.rsqrt(ms + RMS_EPS) * gain) * (1.0 + sc) + sh


def _proj_kernel(x_ref, gain_ref, sc_ref, sh_ref, w_ref, o_ref, h_ref):
    @pl.when(pl.program_id(1) == 0)
    def _():
        h_ref[...] = _norm_mod(x_ref[...], gain_ref[...], sc_ref[0], sh_ref[0]).astype(BF16)

    o_ref[...] = jnp.dot(h_ref[...], w_ref[...], preferred_element_type=F32).astype(o_ref.dtype)


def _project(x2, gain, sc, sh, w, seq, out_dtype, tn):
    t, d = x2.shape
    n = w.shape[1]
    tm = _tile(seq, 512)
    tps = seq // tm
    return pl.pallas_call(
        _proj_kernel,
        grid=(t // tm, n // tn),
        in_specs=[pl.BlockSpec((tm, d), lambda i, j: (i, 0)),
                  pl.BlockSpec((1, d), lambda i, j: (0, 0)),
                  pl.BlockSpec((1, 1, d), lambda i, j: (i // tps, 0, 0)),
                  pl.BlockSpec((1, 1, d), lambda i, j: (i // tps, 0, 0)),
                  pl.BlockSpec((d, tn), lambda i, j: (0, j))],
        out_specs=pl.BlockSpec((tm, tn), lambda i, j: (i, j)),
        out_shape=jax.ShapeDtypeStruct((t, n), out_dtype),
        scratch_shapes=[pltpu.VMEM((tm, d), BF16)],
        compiler_params=_cparams(("arbitrary", "arbitrary")),
        name="proj_rwkv",
    )(x2, gain.reshape(1, d), sc, sh, w)


def _rope_t_kernel(pos_ref, invf_ref, c_ref, s_ref):
    ang = invf_ref[...] * pos_ref[...]
    c_ref[...] = jnp.cos(ang)
    s_ref[...] = jnp.sin(ang)


def _rope_tables_t(positions):
    t = positions.size
    half = ROPE_DIM // 2
    inv_freq = (1.0 / (ROPE_THETA ** (jnp.arange(0, ROPE_DIM, 2, dtype=F32) / ROPE_DIM))).reshape(half, 1)
    tm = _tile(t, 2048)
    blk = pl.BlockSpec((half, tm), lambda i: (0, i))
    return pl.pallas_call(
        _rope_t_kernel,
        grid=(t // tm,),
        in_specs=[pl.BlockSpec((1, tm), lambda i: (0, i)), pl.BlockSpec((half, 1), lambda i: (0, 0))],
        out_specs=[blk, blk],
        out_shape=[jax.ShapeDtypeStruct((half, t), F32)] * 2,
        compiler_params=_cparams(("arbitrary",)),
        name="rope_tables_t",
    )(positions.astype(F32).reshape(1, t), inv_freq)


def _qkv_kernel(x_ref, gain_ref, sc_ref, sh_ref, wqt_ref, wk_ref, wvt_ref, c_ref, s1_ref, s2_ref, ct_ref, st_ref,
                qt_o, k_o, vt_o):
    h = _norm_mod(x_ref[...], gain_ref[...], sc_ref[0], sh_ref[0]).astype(BF16)
    nt = (((1,), (1,)), ((), ()))
    half = ROPE_DIM // 2
    yq = lax.dot_general(wqt_ref[...], h, nt, preferred_element_type=F32) * (DIFF_QK_DIM ** -0.5 * math.log2(math.e))
    ct = ct_ref[...]
    st = st_ref[...]
    pieces = []
    for b0 in range(0, yq.shape[0], DIFF_QK_DIM):
        r1 = yq[b0:b0 + half]
        r2 = yq[b0 + half:b0 + ROPE_DIM]
        pieces += [r1 * ct - r2 * st, r2 * ct + r1 * st, yq[b0 + ROPE_DIM:b0 + DIFF_QK_DIM]]
    qt_o[...] = jnp.concatenate(pieces, axis=0).astype(qt_o.dtype)
    vt_o[...] = lax.dot_general(wvt_ref[...], h, nt, preferred_element_type=F32).astype(vt_o.dtype)
    yk = jnp.dot(h, wk_ref[...], preferred_element_type=F32)
    c = c_ref[...]
    s1 = s1_ref[...]
    s2 = s2_ref[...]
    for s in range(yk.shape[1] // LANES):
        ys = yk[:, s * LANES:(s + 1) * LANES]
        o = ys * c + pltpu.roll(ys, LANES - half, 1) * s1 + pltpu.roll(ys, half, 1) * s2
        k_o[:, s * LANES:(s + 1) * LANES] = o.astype(k_o.dtype)


def _project_qkv(x2, gain, sc, sh, wq_t, wk, wv_t, seq, rope_tabs, rope_tabs_t):
    t, d = x2.shape
    n = DIFF_WIDTH
    tm = _tile(seq, 512)
    tps = seq // tm
    half = ROPE_DIM // 2
    whole = lambda shape: pl.BlockSpec(shape, lambda i: (0, 0))
    mod = pl.BlockSpec((1, 1, d), lambda i: (i // tps, 0, 0))
    tab = pl.BlockSpec((tm, LANES), lambda i: (i, 0))
    tab_t = pl.BlockSpec((half, tm), lambda i: (0, i))
    return pl.pallas_call(
        _qkv_kernel,
        grid=(t // tm,),
        in_specs=[pl.BlockSpec((tm, d), lambda i: (i, 0)), whole((1, d)), mod, mod,
                  whole((n, d)), whole((d, n)), whole((n, d)), tab, tab, tab, tab_t, tab_t],
        out_specs=[pl.BlockSpec((n, tm), lambda i: (0, i)),
                   pl.BlockSpec((tm, n), lambda i: (i, 0)),
                   pl.BlockSpec((n, tm), lambda i: (0, i))],
        out_shape=[jax.ShapeDtypeStruct((n, t), BF16), jax.ShapeDtypeStruct((t, n), BF16),
                   jax.ShapeDtypeStruct((n, t), BF16)],
        compiler_params=_cparams(("arbitrary",)),
        name="proj_qkv",
    )(x2, gain.reshape(1, d), sc, sh, wq_t, wk, wv_t, *rope_tabs, *rope_tabs_t)


def _attn_kernel(qt_ref, k_ref, vt_ref, lam_ref, gain_ref, o_ref, *, tk, out_scale):
    tq = qt_ref.shape[1]
    seq = k_ref.shape[0]
    cols = 2 * tq
    q = qt_ref[...]
    row = lax.broadcasted_iota(jnp.int32, q.shape, 0)
    zero = jnp.zeros_like(q)
    qbd = jnp.concatenate([jnp.where(row < DIFF_QK_DIM, q, zero),
                           jnp.where(row >= DIFF_QK_DIM, q, zero)], axis=1)

    def scores(c):
        return jnp.dot(k_ref[c * tk:(c + 1) * tk, :], qbd, preferred_element_type=F32)

    nkv = seq // tk
    m = jnp.full((1, cols), -1e30, F32)
    l = jnp.zeros((1, cols), F32)
    acc = jnp.zeros((DIFF_V_DIM, cols), F32)
    s_next = scores(0)
    for c in range(nkv):
        s = s_next
        if c + 1 < nkv:
            s_next = scores(c + 1)
        m_new = jnp.maximum(m, jnp.max(s, axis=0, keepdims=True))
        alpha = jnp.exp2(m - m_new)
        p = jnp.exp2(s - m_new)
        l = alpha * l + jnp.sum(p, axis=0, keepdims=True)
        acc = alpha * acc + jnp.dot(vt_ref[:, c * tk:(c + 1) * tk], p.astype(BF16), preferred_element_type=F32)
        m = m_new
    o = acc / l
    d = o[:, :tq] - lam_ref[...] * o[:, tq:]
    ms = jnp.mean(d * d, axis=0, keepdims=True)
    y = (d * lax.rsqrt(ms + DIFF_NORM_EPS) * gain_ref[...]) * out_scale
    o_ref[...] = jnp.transpose(y).astype(o_ref.dtype)


def _diff_attention(qt, k, vt, lam, sub_gain, lam_init, batch, seq):
    t = k.shape[0]
    tq = _tile(seq, 256)
    tk = _tile(seq, 1024)
    nq = seq // tq
    dv = DIFF_V_DIM
    return pl.pallas_call(
        functools.partial(_attn_kernel, tk=tk, out_scale=1.0 - lam_init),
        grid=(batch, DIFF_HEADS, nq),
        in_specs=[pl.BlockSpec((dv, tq), lambda b, hh, qi: (hh, b * nq + qi)),
                  pl.BlockSpec((seq, dv), lambda b, hh, qi: (b, hh)),
                  pl.BlockSpec((dv, seq), lambda b, hh, qi: (hh, b)),
                  pl.BlockSpec((dv, 1), lambda b, hh, qi: (0, 0)),
                  pl.BlockSpec((dv, 1), lambda b, hh, qi: (0, 0))],
        out_specs=pl.BlockSpec((tq, dv), lambda b, hh, qi: (b * nq + qi, hh)),
        out_shape=jax.ShapeDtypeStruct((t, DIFF_WIDTH), BF16),
        compiler_params=_cparams(("arbitrary", "arbitrary", "arbitrary")),
        name="diff_attn",
    )(qt, k, vt, jnp.broadcast_to(lam.astype(F32).reshape(1, 1), (dv, 1)),
      sub_gain.astype(F32).reshape(dv, 1))


def _head_ones(n):
    r = lax.broadcasted_iota(jnp.int32, (n, n), 0) >> 6
    c = lax.broadcasted_iota(jnp.int32, (n, n), 1) >> 6
    return jnp.where(r == c, 1.0, 0.0).astype(BF16)


def _head_sum(x):
    ones = _head_ones(LANES)
    outs = []
    for s in range(x.shape[1] // LANES):
        xs = x[:, s * LANES:(s + 1) * LANES]
        hi = xs.astype(BF16)
        lo = (xs - hi.astype(F32)).astype(BF16)
        outs.append(jnp.dot(hi, ones, preferred_element_type=F32) + jnp.dot(lo, ones, preferred_element_type=F32))
    return jnp.concatenate(outs, axis=1)


def _rwkv_prep_kernel(p_ref, pp_ref, pn_ref, mup_ref, mun_ref, wdec_ref, w0_ref, wa_ref, a0_ref, gup_ref, kkw_ref,
                      r_o, k_o, v_o, kk_o, lw0_o, lw1_o, a0_o, a1_o, g_o, *, tiles_per_seq):
    i = pl.program_id(0)
    tm = p_ref.shape[0]
    w = RWKV_WIDTH
    first = (i % tiles_per_seq) == 0
    last = (i % tiles_per_seq) == tiles_per_seq - 1

    def shifted(c0, c1):
        p = p_ref[:, c0:c1]
        prow = jnp.where(first, 0.0, pp_ref[7:8, c0:c1])
        nrow = jnp.where(last, 0.0, pn_ref[0:1, c0:c1])
        row = lax.broadcasted_iota(jnp.int32, p.shape, 0)
        prev = jnp.where(row == 0, prow, pltpu.roll(p, 1, 0))
        nxt = jnp.where(row == tm - 1, nrow, pltpu.roll(p, tm - 1, 0))
        return p + mup_ref[:, c0:c1] * (prev - p) + mun_ref[:, c0:c1] * (nxt - p)

    lo = shifted(3 * w, 3 * w + LORA_COLS)
    wd = jnp.tanh(lo[:, :2 * DECAY_LORA]).astype(BF16)
    ad = lo[:, 2 * DECAY_LORA:2 * DECAY_LORA + 2 * ICLR_LORA].astype(BF16)
    gd = jax.nn.sigmoid(lo[:, 2 * DECAY_LORA + 2 * ICLR_LORA:]).astype(BF16)
    lw = -DECAY_SCALE * jax.nn.sigmoid(w0_ref[...] + jnp.dot(wd, wdec_ref[...], preferred_element_type=F32))
    lw0_o[...] = lw[:, :w]
    lw1_o[...] = lw[:, w:]
    a = jax.nn.sigmoid(a0_ref[...] + jnp.dot(ad, wa_ref[...], preferred_element_type=F32))
    a0_o[...] = a[:, :w].astype(a0_o.dtype)
    a1_o[...] = a[:, w:].astype(a1_o.dtype)
    g_o[...] = jnp.dot(gd, gup_ref[...], preferred_element_type=F32).astype(g_o.dtype)
    r_o[...] = shifted(0, w).astype(r_o.dtype)
    k = shifted(w, 2 * w)
    k_o[...] = k.astype(k_o.dtype)
    v_o[...] = shifted(2 * w, 3 * w).astype(v_o.dtype)
    kk = k * kkw_ref[...]
    nrm = jnp.sqrt(_head_sum(kk * kk))
    kk_o[...] = (kk / jnp.maximum(nrm, L2_EPS)).astype(kk_o.dtype)


def _rwkv_prep(p, mu_prev, mu_next, decay_w0, decay_up, iclr_a0, iclr_up, k_k, gate_up, seq):
    t = p.shape[0]
    w = RWKV_WIDTH
    tm = _tile(seq, 256)
    tps = seq // tm
    nblk8 = t // 8
    z = jnp.zeros((DECAY_LORA, w), F32)
    wdec = jnp.concatenate([jnp.concatenate([decay_up[0], z], 0), jnp.concatenate([z, decay_up[1]], 0)], 1).astype(BF16)
    wa = jnp.concatenate([jnp.concatenate([iclr_up[0], z], 0), jnp.concatenate([z, iclr_up[1]], 0)], 1).astype(BF16)
    full = lambda shape: pl.BlockSpec(shape, lambda i: (0,) * len(shape))
    blk = pl.BlockSpec((tm, w), lambda i: (i, 0))
    outs = pl.pallas_call(
        functools.partial(_rwkv_prep_kernel, tiles_per_seq=tps),
        grid=(t // tm,),
        in_specs=[pl.BlockSpec((tm, RWKV_COLS), lambda i: (i, 0)),
                  pl.BlockSpec((8, RWKV_COLS), lambda i: (jnp.maximum(i * (tm // 8) - 1, 0), 0)),
                  pl.BlockSpec((8, RWKV_COLS), lambda i: (jnp.minimum((i + 1) * (tm // 8), nblk8 - 1), 0)),
                  full((1, RWKV_COLS)), full((1, RWKV_COLS)),
                  full((2 * DECAY_LORA, 2 * w)), full((1, 2 * w)),
                  full((2 * ICLR_LORA, 2 * w)), full((1, 2 * w)),
                  full((GATE_LORA, w)), full((1, w))],
        out_specs=[blk] * 9,
        out_shape=([jax.ShapeDtypeStruct((t, w), BF16)] * 4 + [jax.ShapeDtypeStruct((t, w), F32)] * 2
                   + [jax.ShapeDtypeStruct((t, w), BF16)] * 3),
        compiler_params=_cparams(("arbitrary",)),
        name="rwkv_prep",
    )(p, p, p, mu_prev.reshape(1, -1), mu_next.reshape(1, -1), wdec, decay_w0.reshape(1, 2 * w),
      wa, iclr_a0.reshape(1, 2 * w), gate_up.astype(BF16), k_k.reshape(1, w))
    return outs


def _split2(x):
    hi = x.astype(BF16)
    return hi, (x - hi.astype(F32)).astype(BF16)


def _scan_chunks(chains):
    c, g = chains[0][0].shape
    nh = g // RWKV_HEAD
    pc = nh * c
    lp = dict(preferred_element_type=F32)
    nt = (((1,), (1,)), ((), ()))
    tn = (((0,), (0,)), ((), ()))
    fwds = [ch[7] for ch in chains]
    n = len(chains)
    each = lambda f, *ls: [f(*xs) for xs in zip(*ls)]

    ti = lax.broadcasted_iota(jnp.int32, (c, c), 0)
    si = lax.broadcasted_iota(jnp.int32, (c, c), 1)
    tp = lax.broadcasted_iota(jnp.int32, (c, pc), 0)
    sp = lax.broadcasted_iota(jnp.int32, (c, pc), 1) & (c - 1)
    tri = {f: jnp.where((si <= ti) if f else (si >= ti), 1.0, 0.0).astype(BF16) for f in set(fwds)}
    strict = {f: (sp < tp) if f else (sp > tp) for f in set(fwds)}
    incl = {f: (sp <= tp) if f else (sp >= tp) for f in set(fwds)}
    eye = jnp.where(sp == tp, 1.0, 0.0).astype(F32)
    lane_head = lax.broadcasted_iota(jnp.int32, (c, g), 1) >> 6
    rh = lax.broadcasted_iota(jnp.int32, (g, g), 0) >> 6
    chh = lax.broadcasted_iota(jnp.int32, (g, g), 1) >> 6

    def bd(x):
        z = jnp.zeros_like(x)
        return jnp.concatenate([jnp.where(lane_head == h, x, z) for h in range(nh)], axis=0)

    def mm(x, y):
        return jnp.dot(x.astype(BF16), bd(y.astype(BF16)), **lp)

    r, k, v, kk, lw, a = [[ch[i] for ch in chains] for i in range(6)]
    st_refs = [ch[6] for ch in chains]

    def cum(f, lw_):
        hi, lo = _split2(lw_)
        return jnp.dot(tri[f], hi, **lp) + jnp.dot(tri[f], lo, **lp)

    lc = each(cum, fwds, lw)
    ltot = each(lambda f, x: x[c - 1:c, :] if f else x[0:1, :], fwds, lc)
    e_inc = each(jnp.exp, lc)
    e_inv = each(lambda x: jnp.exp(-x), lc)
    e_rem = each(lambda lt, x: jnp.exp(lt - x), ltot, lc)
    kd = each(lambda k_, a_, ka_: k_ * (1.0 + (a_ - 1.0) * ka_), k, a, [ch[8] for ch in chains])
    b = each(lambda kk_, a_: kk_ * a_, kk, a)
    a_t = each(lambda kk_, lc_, lw_: (-kk_ * jnp.exp(lc_ - lw_)).astype(BF16), kk, lc, lw)
    r_t = each(lambda r_, e: (r_ * e).astype(BF16), r, e_inc)
    b_t = each(lambda b_, e: (b_ * e).astype(BF16), b, e_inv)
    k_t = each(lambda kd_, e: (kd_ * e).astype(BF16), kd, e_inv)
    b_h = each(lambda b_, e: (b_ * e).astype(BF16), b, e_rem)
    k_h = each(lambda kd_, e: (kd_ * e).astype(BF16), kd, e_rem)
    vb = each(lambda v_: v_.astype(BF16), v)

    ar = each(lambda x, y: jnp.concatenate([x, y], axis=0), a_t, r_t)
    m = each(lambda ar_, b_, k_: lax.dot_general(ar_, jnp.concatenate([bd(b_), bd(k_)], axis=0), nt, **lp),
             ar, b_t, k_t)
    n1 = each(lambda f, m_: jnp.where(strict[f], m_[:c, :pc], 0.0), fwds, m)
    m_ak = each(lambda f, m_: jnp.where(strict[f], m_[:c, pc:], 0.0).astype(BF16), fwds, m)
    m_r = each(lambda f, m_: jnp.concatenate([jnp.where(incl[f], m_[c:, :pc], 0.0),
                                              jnp.where(incl[f], m_[c:, pc:], 0.0)], axis=1).astype(BF16), fwds, m)

    pinv = each(lambda x: eye + x, n1)
    nk = each(mm, n1, n1)
    for _ in range(int(math.log2(c)) - 2):
        rr = each(lambda p_, nk_: mm(jnp.concatenate([p_, nk_], axis=0), nk_), pinv, nk)
        pinv = each(lambda p_, rr_: p_ + rr_[:c], pinv, rr)
        nk = [rr_[c:] for rr_ in rr]
    pinv = each(lambda p_, nk_: p_ + mm(p_, nk_), pinv, nk)

    st = [ref[...] for ref in st_refs]
    xs = each(lambda ar_, st_: lax.dot_general(ar_, st_.astype(BF16), nt, **lp), ar, st)
    x1 = each(lambda xs_, mak, vb_: xs_[:c] + jnp.dot(mak, bd(vb_), **lp), xs, m_ak, vb)
    ub = each(lambda p_, x_: mm(p_, x_).astype(BF16), pinv, x1)
    y = each(lambda xs_, mr, ub_, vb_: xs_[c:] + jnp.dot(mr, jnp.concatenate([bd(ub_), bd(vb_)], axis=0), **lp),
             xs, m_r, ub, vb)
    ds = each(lambda ub_, vb_, bh, kh: lax.dot_general(jnp.concatenate([ub_, vb_], axis=0),
                                                       jnp.concatenate([bh, kh], axis=0), tn, **lp),
              ub, vb, b_h, k_h)
    for i in range(n):
        st_refs[i][...] = st[i] * jnp.exp(ltot[i]) + jnp.where(rh == chh, ds[i], 0.0)
    return y


def _scan_kernel(rf, kf, vf, kkf, lwf, af, rb, kb, vb, kkb, lwb, ab, ka_ref, yf_o, yb_o, stf, stb):
    @pl.when(pl.program_id(1) == 0)
    def _():
        stf[...] = jnp.zeros_like(stf)
        stb[...] = jnp.zeros_like(stb)

    nb = rf.shape[0]
    g = SCAN_GROUP
    halves = rf.shape[2] // g
    chains, outs = [], []
    for hh in range(halves):
        cs = slice(hh * g, (hh + 1) * g)
        ld = lambda ref, b: ref[b, :, cs].astype(F32)
        ka = ka_ref[:, cs]
        for b in range(nb):
            chains.append((ld(rf, b), ld(kf, b), ld(vf, b), ld(kkf, b), ld(lwf, b), ld(af, b),
                           stf.at[hh * nb + b], True, ka))
            outs.append((yf_o, b, cs))
            chains.append((ld(rb, b), ld(kb, b), ld(vb, b), ld(kkb, b), ld(lwb, b), ld(ab, b),
                           stb.at[hh * nb + b], False, ka))
            outs.append((yb_o, b, cs))
    ys = _scan_chunks(chains)
    for (o_ref, b, cs), y in zip(outs, ys):
        o_ref[b, :, cs] = y


def _rwkv_scan(r, k, v, kk, lw0, lw1, a0, a1, k_a, batch, seq):
    t, w = r.shape
    c = _tile(seq, SCAN_CHUNK)
    g = SCAN_GROUP
    gw = 2 * g
    nc = seq // c
    fwd = pl.BlockSpec((batch, c, gw), lambda gi, ci: (0, ci, gi))
    bwd = pl.BlockSpec((batch, c, gw), lambda gi, ci: (0, nc - 1 - ci, gi))
    b3 = lambda z: z.reshape(batch, seq, w)
    nst = batch * (gw // g)
    yf, yb = pl.pallas_call(
        _scan_kernel,
        grid=(w // gw, nc),
        in_specs=[fwd] * 6 + [bwd] * 6 + [pl.BlockSpec((1, gw), lambda gi, ci: (0, gi))],
        out_specs=[fwd, bwd],
        out_shape=[jax.ShapeDtypeStruct((batch, seq, w), F32)] * 2,
        scratch_shapes=[pltpu.VMEM((nst, g, g), F32), pltpu.VMEM((nst, g, g), F32)],
        compiler_params=_cparams(("arbitrary", "arbitrary")),
        name="rwkv_scan",
    )(b3(r), b3(k), b3(v), b3(kk), b3(lw0), b3(a0), b3(r), b3(k), b3(v), b3(kk), b3(lw1), b3(a1), k_a.reshape(1, w))
    return yf.reshape(t, w), yb.reshape(t, w)


def _rwkv_post_kernel(yf, yb, r, k, v, a0, a1, g, ka, rk, gg, gb, o_ref):
    inv = 1.0 / RWKV_HEAD
    y = yf[...] + yb[...]
    mean = _head_sum(y) * inv
    yc = y - mean
    var = _head_sum(yc * yc) * inv
    yn = yc * lax.rsqrt(var + GN_EPS) * gg[...] + gb[...]
    f = lambda ref: ref[...].astype(F32)
    ksum = f(k) * (2.0 + (f(a0) + f(a1) - 2.0) * ka[...])
    bonus = _head_sum(f(r) * ksum * rk[...]) * f(v)
    o_ref[...] = ((yn + bonus) * f(g)).astype(o_ref.dtype)


def _rwkv_post(yf, yb, r, k, v, a0, a1, g, k_a, r_k, gn_gain, gn_bias):
    t, w = yf.shape
    tm = _tile(t, 256)
    blk = pl.BlockSpec((tm, w), lambda i: (i, 0))
    row = pl.BlockSpec((1, w), lambda i: (0, 0))
    return pl.pallas_call(
        _rwkv_post_kernel,
        grid=(t // tm,),
        in_specs=[blk] * 8 + [row] * 4,
        out_specs=blk,
        out_shape=jax.ShapeDtypeStruct((t, w), BF16),
        compiler_params=_cparams(("arbitrary",)),
        name="rwkv_post",
    )(yf, yb, r, k, v, a0, a1, g, k_a.reshape(1, w), r_k.reshape(1, w), gn_gain.reshape(1, w), gn_bias.reshape(1, w))


def _route_tile(sc, sel):
    rows = lambda m: [m[e:e + 1, :] for e in range(N_EXPERTS)]
    sel_r = rows(sel)
    sc_r = rows(sc)
    npg = EXPERTS_PER_GROUP

    def top2_sum(a, b, c, d):
        return jnp.maximum(jnp.maximum(jnp.maximum(a + b, a + c), jnp.maximum(a + d, b + c)),
                           jnp.maximum(b + d, c + d))

    best = top2_sum(*sel_r[0:npg])
    grp = jnp.zeros_like(best, dtype=jnp.int32)
    for gi in range(1, N_GROUPS):
        gs = top2_sum(*sel_r[gi * npg:(gi + 1) * npg])
        upd = gs > best
        grp = jnp.where(upd, gi, grp)
        best = jnp.where(upd, gs, best)

    def in_group(r, j):
        out = r[(N_GROUPS - 1) * npg + j]
        for gi in range(N_GROUPS - 2, -1, -1):
            out = jnp.where(grp == gi, r[gi * npg + j], out)
        return out

    v = [in_group(sel_r, j) for j in range(npg)]
    s = [in_group(sc_r, j) for j in range(npg)]
    b1, i1, s1 = v[0], jnp.zeros_like(grp), s[0]
    for j in range(1, npg):
        upd = v[j] > b1
        b1 = jnp.where(upd, v[j], b1)
        i1 = jnp.where(upd, j, i1)
        s1 = jnp.where(upd, s[j], s1)
    neg = jnp.full_like(b1, -jnp.inf)
    b2, i2, s2 = neg, jnp.zeros_like(grp), s[0]
    for j in range(npg):
        upd = jnp.where(i1 == j, neg, v[j]) > b2
        b2 = jnp.where(upd, v[j], b2)
        i2 = jnp.where(upd, j, i2)
        s2 = jnp.where(upd, s[j], s2)
    den = s1 + s2
    return grp * npg + i1, grp * npg + i2, s1 / den, s2 / den


def _outproj_kernel(yd_ref, yr_ref, wo_ref, x_ref, g1_ref, gain_ref, sc_ref, sh_ref, wrt_ref, bias_ref,
                    x_o, hp_o, eid_o, rank_o, gate_o, cnt_o, run_ref):
    @pl.when(pl.program_id(0) == 0)
    def _():
        run_ref[...] = jnp.zeros_like(run_ref)

    tm = x_ref.shape[0]
    half = DIFF_WIDTH
    mix = (jnp.dot(yd_ref[...], wo_ref[:half, :], preferred_element_type=F32)
           + jnp.dot(yr_ref[...], wo_ref[half:, :], preferred_element_type=F32))
    xn = x_ref[...] + g1_ref[0] * mix
    x_o[...] = xn
    hb = _norm_mod(xn, gain_ref[...], sc_ref[0], sh_ref[0]).astype(BF16)
    u = pltpu.bitcast(hb.astype(F32), U32)
    hd = D_MODEL // 2
    hp_o[...] = (u[:, :hd] & jnp.uint32(0xFFFF0000)) | (u[:, hd:] >> 16)

    logits = lax.dot_general(wrt_ref[...], hb, (((1,), (1,)), ((), ())), preferred_element_type=F32)
    sc = jax.nn.sigmoid(logits)
    e1, e2, g1, g2 = _route_tile(sc, sc + bias_ref[...])
    eid_o[0:1, :] = e1
    eid_o[1:2, :] = e2
    eio = lax.broadcasted_iota(jnp.int32, (N_EXPERTS, tm), 0)
    oh1 = eio == e1
    oh2 = eio == e2
    oh = jnp.where(oh1, 1.0, 0.0) + jnp.where(oh2, 1.0, 0.0)
    before = (lax.broadcasted_iota(jnp.int32, (tm, tm), 0) < lax.broadcasted_iota(jnp.int32, (tm, tm), 1))
    base = (jnp.dot(oh.astype(BF16), jnp.where(before, 1.0, 0.0).astype(BF16), preferred_element_type=F32)
            + run_ref[:, 0:1])
    rank_o[0:1, :] = jnp.sum(jnp.where(oh1, base, 0.0), axis=0, keepdims=True).astype(jnp.int32)
    rank_o[1:2, :] = jnp.sum(jnp.where(oh2, base, 0.0), axis=0, keepdims=True).astype(jnp.int32)
    run = run_ref[...] + jnp.sum(oh, axis=1, keepdims=True)
    run_ref[...] = run
    cnt_o[...] = run
    ri = lax.broadcasted_iota(jnp.int32, (LANES, tm), 0)
    gate_o[...] = jnp.transpose(jnp.where(ri == 0, g1, jnp.where(ri == 1, g2, 0.0)))


def _out_proj(yd, yr, w_out, x2, g1, gain, sc, sh, w_router, router_bias, seq):
    t, d = x2.shape
    tm = _tile(seq, 256)
    tps = seq // tm
    mod = pl.BlockSpec((1, 1, d), lambda i: (i // tps, 0, 0))
    ne = N_EXPERTS
    return pl.pallas_call(
        _outproj_kernel,
        grid=(t // tm,),
        in_specs=[pl.BlockSpec((tm, DIFF_WIDTH), lambda i: (i, 0)),
                  pl.BlockSpec((tm, RWKV_WIDTH), lambda i: (i, 0)),
                  pl.BlockSpec((d, d), lambda i: (0, 0)),
                  pl.BlockSpec((tm, d), lambda i: (i, 0)),
                  mod,
                  pl.BlockSpec((1, d), lambda i: (0, 0)),
                  mod, mod,
                  pl.BlockSpec((ne, d), lambda i: (0, 0)),
                  pl.BlockSpec((ne, 1), lambda i: (0, 0))],
        out_specs=[pl.BlockSpec((tm, d), lambda i: (i, 0)),
                   pl.BlockSpec((tm, d // 2), lambda i: (i, 0)),
                   pl.BlockSpec((TOP_K, tm), lambda i: (0, i)),
                   pl.BlockSpec((TOP_K, tm), lambda i: (0, i)),
                   pl.BlockSpec((tm, LANES), lambda i: (i, 0)),
                   pl.BlockSpec((ne, LANES), lambda i: (0, 0))],
        out_shape=[jax.ShapeDtypeStruct((t, d), F32),
                   jax.ShapeDtypeStruct((t, d // 2), U32),
                   jax.ShapeDtypeStruct((TOP_K, t), jnp.int32),
                   jax.ShapeDtypeStruct((TOP_K, t), jnp.int32),
                   jax.ShapeDtypeStruct((t, LANES), F32),
                   jax.ShapeDtypeStruct((ne, LANES), F32)],
        scratch_shapes=[pltpu.VMEM((ne, LANES), F32)],
        compiler_params=_cparams(("arbitrary",)),
        name="out_proj",
    )(yd, yr, w_out, x2, g1, gain.reshape(1, d), sc, sh, w_router.T.astype(BF16),
      router_bias.astype(F32).reshape(ne, 1))


def _expert_layout(counts, n_blocks):
    counts = counts.astype(jnp.int32)
    padded = ((counts + MOE_BM - 1) // MOE_BM) * MOE_BM
    padded_ends = jnp.cumsum(padded)
    offsets = (padded_ends - padded).astype(jnp.int32)
    block_start = jnp.arange(n_blocks, dtype=jnp.int32) * MOE_BM
    block_e = jnp.minimum(jnp.sum((block_start[:, None] >= padded_ends[None, :]).astype(jnp.int32), axis=1),
                          N_EXPERTS - 1).astype(jnp.int32)
    n_used = (padded_ends[-1] // MOE_BM).astype(jnp.int32).reshape(1)
    return offsets, block_e, n_used


def _dispatch_kernel(dest_ref, h_ref, xin_ref, xr_ref, hbuf, sem_in, sem_out, *, tm):
    del xin_ref
    i = pl.program_id(0)
    n = pl.num_programs(0)
    t = dest_ref.shape[0] // TOP_K
    slot = i % 2

    def load(tile, s):
        return pltpu.make_async_copy(h_ref.at[pl.ds(tile * tm, tm), :], hbuf.at[s], sem_in.at[s])

    def wait_out(s):
        for kx in range(TOP_K):
            pltpu.make_async_copy(hbuf.at[s], xr_ref.at[pl.ds(0, tm), :], sem_out.at[s]).wait()

    @pl.when(i == 0)
    def _():
        load(0, 0).start()

    load(i, slot).wait()

    def issue(r, carry):
        for kx in range(TOP_K):
            d = dest_ref[kx * t + i * tm + r]
            pltpu.make_async_copy(hbuf.at[slot, pl.ds(r, 1), :], xr_ref.at[pl.ds(d, 1), :], sem_out.at[slot]).start()
        return carry

    lax.fori_loop(0, tm, issue, 0, unroll=8)

    @pl.when(i > 0)
    def _():
        wait_out(1 - slot)

    @pl.when(i + 1 < n)
    def _():
        load(i + 1, 1 - slot).start()

    @pl.when(i == n - 1)
    def _():
        wait_out(slot)


def _dispatch(hp, dest, n_rows):
    t, hd = hp.shape
    tm = _tile(t, 256)
    zeros = jnp.zeros((n_rows, hd), U32)
    return pl.pallas_call(
        functools.partial(_dispatch_kernel, tm=tm),
        grid_spec=pltpu.PrefetchScalarGridSpec(
            num_scalar_prefetch=1,
            grid=(t // tm,),
            in_specs=[pl.BlockSpec(memory_space=pl.ANY),
                      pl.BlockSpec(memory_space=pl.ANY)],
            out_specs=pl.BlockSpec(memory_space=pl.ANY),
            scratch_shapes=[pltpu.VMEM((2, tm, hd), U32), pltpu.SemaphoreType.DMA((2,)),
                            pltpu.SemaphoreType.DMA((2,))]),
        out_shape=jax.ShapeDtypeStruct((n_rows, hd), U32),
        input_output_aliases={2: 0},
        compiler_params=_cparams(("arbitrary",)),
        name="moe_dispatch",
    )(dest, hp, zeros)


FFN_STAGE_CHUNKS = 8


def _ffn_kernel(be_ref, nu_ref, x_ref, wg_hbm, wu_hbm, wd_hbm, o_ref, wg_s, wu_s, wd_s, st_a, st_b, sem_a, sem_b,
                *, layer):
    i = pl.program_id(0)
    e = be_ref[i]
    active = i < nu_ref[0]
    fresh = jnp.logical_and(active, jnp.logical_or(i == 0, e != be_ref[jnp.maximum(i - 1, 0)]))

    @pl.when(fresh)
    def _():
        jobs = []
        for src, dst, stage, sem in ((wg_hbm, wg_s, st_a, sem_a), (wu_hbm, wu_s, st_a, sem_a),
                                     (wd_hbm, wd_s, st_b, sem_b)):
            rows = dst.shape[0] // FFN_STAGE_CHUNKS
            for c in range(FFN_STAGE_CHUNKS):
                jobs.append((src, dst, stage, sem, rows, c, c % 2))

        def copy(j):
            src, _, stage, sem, rows, c, slot = jobs[j]
            return pltpu.make_async_copy(src.at[layer, e, pl.ds(c * rows, rows), :], stage.at[slot], sem.at[slot])

        copy(0).start()
        for j in range(len(jobs)):
            if j + 1 < len(jobs):
                copy(j + 1).start()
            copy(j).wait()
            _, dst, stage, _, rows, c, slot = jobs[j]
            dst[c * rows:(c + 1) * rows, :] = stage[slot].astype(BF16)

    @pl.when(active)
    def _():
        xp = x_ref[...]
        xa = pltpu.bitcast(xp & jnp.uint32(0xFFFF0000), F32).astype(BF16)
        xb = pltpu.bitcast(xp << 16, F32).astype(BF16)
        x = jnp.concatenate([xa, xb], axis=1)
        hg = jnp.dot(x, wg_s[...], preferred_element_type=F32)
        hu = jnp.dot(x, wu_s[...], preferred_element_type=F32)
        h = (hg * jax.nn.sigmoid(hg) * hu).astype(BF16)
        o_ref[...] = jnp.dot(h, wd_s[...], preferred_element_type=F32)

    @pl.when(jnp.logical_not(active))
    def _():
        o_ref[...] = jnp.zeros_like(o_ref)


def _expert_ffn(xr, block_e, n_used, wg, wu, wd, layer):
    p, hd = xr.shape
    d = 2 * hd
    nb = p // MOE_BM
    sc = FFN_STAGE_CHUNKS
    hbm = pl.BlockSpec(memory_space=pl.ANY)
    return pl.pallas_call(
        functools.partial(_ffn_kernel, layer=layer),
        grid_spec=pltpu.PrefetchScalarGridSpec(
            num_scalar_prefetch=2,
            grid=(nb,),
            in_specs=[pl.BlockSpec((MOE_BM, hd), lambda i, be, nu: (i, 0)), hbm, hbm, hbm],
            out_specs=pl.BlockSpec((MOE_BM, d), lambda i, be, nu: (i, 0)),
            scratch_shapes=[pltpu.VMEM((d, D_FF), BF16), pltpu.VMEM((d, D_FF), BF16), pltpu.VMEM((D_FF, d), BF16),
                            pltpu.VMEM((2, d // sc, D_FF), F32), pltpu.VMEM((2, D_FF // sc, d), F32),
                            pltpu.SemaphoreType.DMA((2,)), pltpu.SemaphoreType.DMA((2,))]),
        out_shape=jax.ShapeDtypeStruct((p, d), F32),
        compiler_params=_cparams(("arbitrary",)),
        name="moe_ffn",
    )(block_e, n_used, xr, wg, wu, wd)


def _combine_kernel(dest_ref, yr_ref, gate_ref, x_ref, g2_ref, fg_ref, o_ref, buf, sem, *, final):
    i = pl.program_id(0)
    n = pl.num_programs(0)
    tm = x_ref.shape[0]
    t = dest_ref.shape[0] // TOP_K

    def issue_tile(tile, slot):
        def issue(r, carry):
            for kx in range(TOP_K):
                d = dest_ref[kx * t + tile * tm + r]
                pltpu.make_async_copy(yr_ref.at[pl.ds(d, 1), :], buf.at[slot, kx, pl.ds(r, 1), :],
                                      sem.at[slot]).start()
            return carry

        lax.fori_loop(0, tm, issue, 0, unroll=8)

    @pl.when(i == 0)
    def _():
        issue_tile(0, 0)

    @pl.when(i + 1 < n)
    def _():
        issue_tile(i + 1, (i + 1) % 2)

    slot = i % 2
    for kx in range(TOP_K):
        pltpu.make_async_copy(yr_ref.at[pl.ds(0, tm), :], buf.at[slot, kx], sem.at[slot]).wait()

    gate = gate_ref[...]
    y = gate[:, 0:1] * buf[slot, 0] + gate[:, 1:2] * buf[slot, 1]
    xn = x_ref[...] + g2_ref[0] * y
    if final:
        ms = jnp.mean(xn * xn, axis=-1, keepdims=True)
        xn = xn * lax.rsqrt(ms + RMS_EPS) * fg_ref[...]
    o_ref[...] = xn


def _combine(yr, dest, gate, x2, g2, final_gain, seq, final):
    t, d = x2.shape
    tm = _tile(seq, 256)
    tps = seq // tm
    return pl.pallas_call(
        functools.partial(_combine_kernel, final=final),
        grid_spec=pltpu.PrefetchScalarGridSpec(
            num_scalar_prefetch=1,
            grid=(t // tm,),
            in_specs=[pl.BlockSpec(memory_space=pl.ANY),
                      pl.BlockSpec((tm, LANES), lambda i, *_: (i, 0)),
                      pl.BlockSpec((tm, d), lambda i, *_: (i, 0)),
                      pl.BlockSpec((1, 1, d), lambda i, *_: (i // tps, 0, 0)),
                      pl.BlockSpec((1, d), lambda i, *_: (0, 0))],
            out_specs=pl.BlockSpec((tm, d), lambda i, *_: (i, 0)),
            scratch_shapes=[pltpu.VMEM((2, TOP_K, tm, d), F32), pltpu.SemaphoreType.DMA((2,))]),
        out_shape=jax.ShapeDtypeStruct((t, d), F32),
        compiler_params=_cparams(("arbitrary",)),
        name="moe_combine",
    )(dest, yr, gate, x2, g2, final_gain.reshape(1, d))


def kernel(x, c, positions, w_ada, b_ada, norm_gain, w_in, w_out, mu_prev, mu_next, decay_w0, decay_up, iclr_a0, iclr_up, k_k, k_a, r_k, gate_up, gn_gain, gn_bias, lam_q1, lam_k1, lam_q2, lam_k2, diff_sub_gain, w_router, router_bias, w_gate, w_up, w_down, final_gain):
    batch, seq, d = x.shape
    depth = w_ada.shape[0]
    t = batch * seq
    w = RWKV_WIDTH
    x2 = x.reshape(t, d)
    mod = _ada_mod(c, w_ada, b_ada)
    rope_tabs = _rope_tables(positions)
    rope_tabs_t = _rope_tables_t(positions)
    n_rows = t * TOP_K + N_EXPERTS * MOE_BM
    for l in range(depth):
        sh1, sc1, g1, sh2, sc2, g2 = [mod[l, :, i * d:(i + 1) * d].reshape(batch, 1, d) for i in range(N_MOD)]
        w_l = w_in[l].astype(BF16)
        dw = DIFF_WIDTH
        qt, kq, vt = _project_qkv(x2, norm_gain[l, 0], sc1, sh1, w_l[:, :dw].T, w_l[:, dw:2 * dw],
                                  w_l[:, 2 * dw:3 * dw].T, seq, rope_tabs, rope_tabs_t)
        p = _project(x2, norm_gain[l, 0], sc1, sh1, w_l[:, DIFF_COLS:], seq, F32, 1152)
        lam_init = 0.8 - 0.6 * math.exp(-0.3 * l)
        lam = (jnp.exp(jnp.sum(lam_q1[l].astype(F32) * lam_k1[l].astype(F32)))
               - jnp.exp(jnp.sum(lam_q2[l].astype(F32) * lam_k2[l].astype(F32))) + lam_init)
        y_diff = _diff_attention(qt, kq, vt, lam, diff_sub_gain[l], lam_init, batch, seq)
        r, k, v, kk, lw0, lw1, a0, a1, g = _rwkv_prep(p, mu_prev[l], mu_next[l], decay_w0[l], decay_up[l],
                                                      iclr_a0[l], iclr_up[l], k_k[l], gate_up[l], seq)
        yf, yb = _rwkv_scan(r, k, v, kk, lw0, lw1, a0, a1, k_a[l], batch, seq)
        y_rwkv = _rwkv_post(yf, yb, r, k, v, a0, a1, g, k_a[l], r_k[l], gn_gain[l], gn_bias[l])
        x2, hp, eid, rank, gate, cnt = _out_proj(y_diff, y_rwkv, w_out[l].astype(BF16), x2, g1, norm_gain[l, 1],
                                                 sc2, sh2, w_router, router_bias, seq)
        offsets, block_e, n_used = _expert_layout(cnt[:, 0], n_rows // MOE_BM)
        onehot = eid[:, :, None] == jnp.arange(N_EXPERTS, dtype=jnp.int32)
        dest = (rank + jnp.sum(jnp.where(onehot, offsets, 0), axis=-1)).reshape(TOP_K * t)
        xr = _dispatch(hp, dest, n_rows)
        yr = _expert_ffn(xr, block_e, n_used, w_gate, w_up, w_down, l)
        x2 = _combine(yr, dest, gate, x2, g2, final_gain, seq, final=(l == depth - 1))
    return x2.reshape(batch, seq, d)
```

```python
import functools
import math

import jax
import jax.numpy as jnp
from jax import lax
from jax.experimental import pallas as pl
from jax.experimental.pallas import tpu as pltpu

F32 = jnp.float32
BF16 = jnp.bfloat16
U32 = jnp.uint32
HIGHEST = lax.Precision.HIGHEST

D_MODEL = 2048
DIFF_WIDTH = 1024
RWKV_WIDTH = 1024
DIFF_QK_DIM = 64
DIFF_V_DIM = 128
DIFF_HEADS = 8
ROPE_DIM = 16
ROPE_THETA = 500000.0
DIFF_NORM_EPS = 1e-5
RWKV_HEAD = 64
DECAY_LORA = 64
ICLR_LORA = 64
GATE_LORA = 128
DECAY_SCALE = 0.606531
GN_EPS = 64e-5
L2_EPS = 1e-12
N_EXPERTS = 16
N_GROUPS = 4
EXPERTS_PER_GROUP = 4
TOP_K = 2
D_FF = 1024
RMS_EPS = 1e-6
N_MOD = 6
DIFF_COLS = 3 * DIFF_WIDTH
RWKV_COLS = 3 * RWKV_WIDTH + 2 * DECAY_LORA + 2 * ICLR_LORA + GATE_LORA
LORA_COLS = RWKV_COLS - 3 * RWKV_WIDTH

LANES = 128
SCAN_CHUNK = 64
SCAN_GROUP = 256
MOE_BM = 256
VMEM_LIMIT = 56 * 1024 * 1024


def _cparams(sem):
    return pltpu.CompilerParams(dimension_semantics=sem, vmem_limit_bytes=VMEM_LIMIT)


def _tile(n, pref):
    t = min(n, pref)
    assert n % t == 0, (n, t)
    return t


def _mod_kernel(c_ref, w_ref, b_ref, o_ref):
    c = c_ref[...]
    ca = (c * jax.nn.sigmoid(c)).astype(BF16)
    o_ref[0] = jnp.dot(ca, w_ref[0].astype(BF16), preferred_element_type=F32) + b_ref[0]


def _ada_mod(c, w_ada, b_ada):
    depth, d, n = w_ada.shape
    b = c.shape[0]
    cp = jnp.zeros((8, d), F32).at[:b].set(c)
    tn = 1024
    out = pl.pallas_call(
        _mod_kernel,
        grid=(depth, n // tn),
        in_specs=[pl.BlockSpec((8, d), lambda l, j: (0, 0)),
                  pl.BlockSpec((1, d, tn), lambda l, j: (l, 0, j)),
                  pl.BlockSpec((1, 1, tn), lambda l, j: (l, 0, j))],
        out_specs=pl.BlockSpec((1, 8, tn), lambda l, j: (l, 0, j)),
        out_shape=jax.ShapeDtypeStruct((depth, 8, n), F32),
        compiler_params=_cparams(("arbitrary", "arbitrary")),
        name="ada_mod",
    )(cp, w_ada, b_ada.reshape(depth, 1, n))
    return out[:, :b]


def _rope_kernel(pos_ref, invf_ref, m1_ref, m2_ref, c_ref, s1_ref, s2_ref):
    ang = pos_ref[...] * invf_ref[...]
    s = jnp.sin(ang)
    c_ref[...] = jnp.cos(ang)
    s1_ref[...] = -s * m1_ref[...]
    s2_ref[...] = s * m2_ref[...]


def _rope_tables(positions):
    t = positions.size
    half = ROPE_DIM // 2
    inv_freq = 1.0 / (ROPE_THETA ** (jnp.arange(0, ROPE_DIM, 2, dtype=F32) / ROPE_DIM))
    lane = jnp.arange(LANES) % DIFF_QK_DIM
    m1 = (lane < half).astype(F32)
    m2 = ((lane >= half) & (lane < ROPE_DIM)).astype(F32)
    invf = jnp.where(lane < ROPE_DIM, inv_freq[lane % half], 0.0).astype(F32)
    pos = jnp.broadcast_to(positions.astype(F32).reshape(t, 1), (t, LANES))
    tm = _tile(t, 512)
    row = pl.BlockSpec((1, LANES), lambda i: (0, 0))
    blk = pl.BlockSpec((tm, LANES), lambda i: (i, 0))
    return pl.pallas_call(
        _rope_kernel,
        grid=(t // tm,),
        in_specs=[blk, row, row, row],
        out_specs=[blk, blk, blk],
        out_shape=[jax.ShapeDtypeStruct((t, LANES), F32)] * 3,
        compiler_params=_cparams(("arbitrary",)),
        name="rope_tables",
    )(pos, invf.reshape(1, LANES), m1.reshape(1, LANES), m2.reshape(1, LANES))


def _norm_mod(x, gain, sc, sh):
    ms = jnp.mean(x * x, axis=-1, keepdims=True)
    return (x * lax.rsqrt(ms + RMS_EPS) * gain) * (1.0 + sc) + sh


def _proj_kernel(x_ref, gain_ref, sc_ref, sh_ref, w_ref, o_ref, h_ref):
    @pl.when(pl.program_id(1) == 0)
    def _():
        h_ref[...] = _norm_mod(x_ref[...], gain_ref[...], sc_ref[0], sh_ref[0]).astype(BF16)

    o_ref[...] = jnp.dot(h_ref[...], w_ref[...], preferred_element_type=F32).astype(o_ref.dtype)


def _project(x2, gain, sc, sh, w, seq, out_dtype, tn):
    t, d = x2.shape
    n = w.shape[1]
    tm = _tile(seq, 512)
    tps = seq // tm
    return pl.pallas_call(
        _proj_kernel,
        grid=(t // tm, n // tn),
        in_specs=[pl.BlockSpec((tm, d), lambda i, j: (i, 0)),
                  pl.BlockSpec((1, d), lambda i, j: (0, 0)),
                  pl.BlockSpec((1, 1, d), lambda i, j: (i // tps, 0, 0)),
                  pl.BlockSpec((1, 1, d), lambda i, j: (i // tps, 0, 0)),
                  pl.BlockSpec((d, tn), lambda i, j: (0, j))],
        out_specs=pl.BlockSpec((tm, tn), lambda i, j: (i, j)),
        out_shape=jax.ShapeDtypeStruct((t, n), out_dtype),
        scratch_shapes=[pltpu.VMEM((tm, d), BF16)],
        compiler_params=_cparams(("arbitrary", "arbitrary")),
        name="proj_rwkv",
    )(x2, gain.reshape(1, d), sc, sh, w)


def _rope_t_kernel(pos_ref, invf_ref, c_ref, s_ref):
    ang = invf_ref[...] * pos_ref[...]
    c_ref[...] = jnp.cos(ang)
    s_ref[...] = jnp.sin(ang)


def _rope_tables_t(positions):
    t = positions.size
    half = ROPE_DIM // 2
    inv_freq = (1.0 / (ROPE_THETA ** (jnp.arange(0, ROPE_DIM, 2, dtype=F32) / ROPE_DIM))).reshape(half, 1)
    tm = _tile(t, 2048)
    blk = pl.BlockSpec((half, tm), lambda i: (0, i))
    return pl.pallas_call(
        _rope_t_kernel,
        grid=(t // tm,),
        in_specs=[pl.BlockSpec((1, tm), lambda i: (0, i)), pl.BlockSpec((half, 1), lambda i: (0, 0))],
        out_specs=[blk, blk],
        out_shape=[jax.ShapeDtypeStruct((half, t), F32)] * 2,
        compiler_params=_cparams(("arbitrary",)),
        name="rope_tables_t",
    )(positions.astype(F32).reshape(1, t), inv_freq)


def _qkv_kernel(x_ref, gain_ref, sc_ref, sh_ref, wqt_ref, wk_ref, wvt_ref, c_ref, s1_ref, s2_ref, ct_ref, st_ref,
                qt_o, k_o, vt_o):
    h = _norm_mod(x_ref[...], gain_ref[...], sc_ref[0], sh_ref[0]).astype(BF16)
    nt = (((1,), (1,)), ((), ()))
    half = ROPE_DIM // 2
    yq = lax.dot_general(wqt_ref[...], h, nt, preferred_element_type=F32) * (DIFF_QK_DIM ** -0.5 * math.log2(math.e))
    ct = ct_ref[...]
    st = st_ref[...]
    pieces = []
    for b0 in range(0, yq.shape[0], DIFF_QK_DIM):
        r1 = yq[b0:b0 + half]
        r2 = yq[b0 + half:b0 + ROPE_DIM]
        pieces += [r1 * ct - r2 * st, r2 * ct + r1 * st, yq[b0 + ROPE_DIM:b0 + DIFF_QK_DIM]]
    qt_o[...] = jnp.concatenate(pieces, axis=0).astype(qt_o.dtype)
    vt_o[...] = lax.dot_general(wvt_ref[...], h, nt, preferred_element_type=F32).astype(vt_o.dtype)
    yk = jnp.dot(h, wk_ref[...], preferred_element_type=F32)
    c = c_ref[...]
    s1 = s1_ref[...]
    s2 = s2_ref[...]
    for s in range(yk.shape[1] // LANES):
        ys = yk[:, s * LANES:(s + 1) * LANES]
        o = ys * c + pltpu.roll(ys, LANES - half, 1) * s1 + pltpu.roll(ys, half, 1) * s2
        k_o[:, s * LANES:(s + 1) * LANES] = o.astype(k_o.dtype)


def _project_qkv(x2, gain, sc, sh, wq_t, wk, wv_t, seq, rope_tabs, rope_tabs_t):
    t, d = x2.shape
    n = DIFF_WIDTH
    tm = _tile(seq, 512)
    tps = seq // tm
    half = ROPE_DIM // 2
    whole = lambda shape: pl.BlockSpec(shape, lambda i: (0, 0))
    mod = pl.BlockSpec((1, 1, d), lambda i: (i // tps, 0, 0))
    tab = pl.BlockSpec((tm, LANES), lambda i: (i, 0))
    tab_t = pl.BlockSpec((half, tm), lambda i: (0, i))
    return pl.pallas_call(
        _qkv_kernel,
        grid=(t // tm,),
        in_specs=[pl.BlockSpec((tm, d), lambda i: (i, 0)), whole((1, d)), mod, mod,
                  whole((n, d)), whole((d, n)), whole((n, d)), tab, tab, tab, tab_t, tab_t],
        out_specs=[pl.BlockSpec((n, tm), lambda i: (0, i)),
                   pl.BlockSpec((tm, n), lambda i: (i, 0)),
                   pl.BlockSpec((n, tm), lambda i: (0, i))],
        out_shape=[jax.ShapeDtypeStruct((n, t), BF16), jax.ShapeDtypeStruct((t, n), BF16),
                   jax.ShapeDtypeStruct((n, t), BF16)],
        compiler_params=_cparams(("arbitrary",)),
        name="proj_qkv",
    )(x2, gain.reshape(1, d), sc, sh, wq_t, wk, wv_t, *rope_tabs, *rope_tabs_t)


def _attn_kernel(qt_ref, k_ref, vt_ref, lam_ref, gain_ref, o_ref, *, tk, out_scale):
    tq = qt_ref.shape[1]
    seq = k_ref.shape[0]
    cols = 2 * tq
    q = qt_ref[...]
    row = lax.broadcasted_iota(jnp.int32, q.shape, 0)
    zero = jnp.zeros_like(q)
    qbd = jnp.concatenate([jnp.where(row < DIFF_QK_DIM, q, zero),
                           jnp.where(row >= DIFF_QK_DIM, q, zero)], axis=1)

    def scores(c):
        return jnp.dot(k_ref[c * tk:(c + 1) * tk, :], qbd, preferred_element_type=F32)

    nkv = seq // tk
    m = jnp.full((1, cols), -1e30, F32)
    l = jnp.zeros((1, cols), F32)
    acc = jnp.zeros((DIFF_V_DIM, cols), F32)
    s_next = scores(0)
    for c in range(nkv):
        s = s_next
        if c + 1 < nkv:
            s_next = scores(c + 1)
        m_new = jnp.maximum(m, jnp.max(s, axis=0, keepdims=True))
        alpha = jnp.exp2(m - m_new)
        p = jnp.exp2(s - m_new)
        l = alpha * l + jnp.sum(p, axis=0, keepdims=True)
        acc = alpha * acc + jnp.dot(vt_ref[:, c * tk:(c + 1) * tk], p.astype(BF16), preferred_element_type=F32)
        m = m_new
    o = acc / l
    d = o[:, :tq] - lam_ref[...] * o[:, tq:]
    ms = jnp.mean(d * d, axis=0, keepdims=True)
    y = (d * lax.rsqrt(ms + DIFF_NORM_EPS) * gain_ref[...]) * out_scale
    o_ref[...] = jnp.transpose(y).astype(o_ref.dtype)


def _diff_attention(qt, k, vt, lam, sub_gain, lam_init, batch, seq):
    t = k.shape[0]
    tq = _tile(seq, 256)
    tk = _tile(seq, 1024)
    nq = seq // tq
    dv = DIFF_V_DIM
    return pl.pallas_call(
        functools.partial(_attn_kernel, tk=tk, out_scale=1.0 - lam_init),
        grid=(batch, DIFF_HEADS, nq),
        in_specs=[pl.BlockSpec((dv, tq), lambda b, hh, qi: (hh, b * nq + qi)),
                  pl.BlockSpec((seq, dv), lambda b, hh, qi: (b, hh)),
                  pl.BlockSpec((dv, seq), lambda b, hh, qi: (hh, b)),
                  pl.BlockSpec((dv, 1), lambda b, hh, qi: (0, 0)),
                  pl.BlockSpec((dv, 1), lambda b, hh, qi: (0, 0))],
        out_specs=pl.BlockSpec((tq, dv), lambda b, hh, qi: (b * nq + qi, hh)),
        out_shape=jax.ShapeDtypeStruct((t, DIFF_WIDTH), BF16),
        compiler_params=_cparams(("arbitrary", "arbitrary", "arbitrary")),
        name="diff_attn",
    )(qt, k, vt, jnp.broadcast_to(lam.astype(F32).reshape(1, 1), (dv, 1)),
      sub_gain.astype(F32).reshape(dv, 1))


def _head_ones(n):
    r = lax.broadcasted_iota(jnp.int32, (n, n), 0) >> 6
    c = lax.broadcasted_iota(jnp.int32, (n, n), 1) >> 6
    return jnp.where(r == c, 1.0, 0.0).astype(BF16)


def _head_sum(x):
    ones = _head_ones(LANES)
    outs = []
    for s in range(x.shape[1] // LANES):
        xs = x[:, s * LANES:(s + 1) * LANES]
        hi = xs.astype(BF16)
        lo = (xs - hi.astype(F32)).astype(BF16)
        outs.append(jnp.dot(hi, ones, preferred_element_type=F32) + jnp.dot(lo, ones, preferred_element_type=F32))
    return jnp.concatenate(outs, axis=1)


def _rwkv_prep_kernel(p_ref, pp_ref, pn_ref, mup_ref, mun_ref, wdec_ref, w0_ref, wa_ref, a0_ref, gup_ref, kkw_ref,
                      r_o, k_o, v_o, kk_o, lw0_o, lw1_o, a0_o, a1_o, g_o, *, tiles_per_seq):
    i = pl.program_id(0)
    tm = p_ref.shape[0]
    w = RWKV_WIDTH
    first = (i % tiles_per_seq) == 0
    last = (i % tiles_per_seq) == tiles_per_seq - 1

    def shifted(c0, c1):
        p = p_ref[:, c0:c1]
        prow = jnp.where(first, 0.0, pp_ref[7:8, c0:c1])
        nrow = jnp.where(last, 0.0, pn_ref[0:1, c0:c1])
        row = lax.broadcasted_iota(jnp.int32, p.shape, 0)
        prev = jnp.where(row == 0, prow, pltpu.roll(p, 1, 0))
        nxt = jnp.where(row == tm - 1, nrow, pltpu.roll(p, tm - 1, 0))
        return p + mup_ref[:, c0:c1] * (prev - p) + mun_ref[:, c0:c1] * (nxt - p)

    lo = shifted(3 * w, 3 * w + LORA_COLS)
    wd = jnp.tanh(lo[:, :2 * DECAY_LORA]).astype(BF16)
    ad = lo[:, 2 * DECAY_LORA:2 * DECAY_LORA + 2 * ICLR_LORA].astype(BF16)
    gd = jax.nn.sigmoid(lo[:, 2 * DECAY_LORA + 2 * ICLR_LORA:]).astype(BF16)
    lw = -DECAY_SCALE * jax.nn.sigmoid(w0_ref[...] + jnp.dot(wd, wdec_ref[...], preferred_element_type=F32))
    lw0_o[...] = lw[:, :w]
    lw1_o[...] = lw[:, w:]
    a = jax.nn.sigmoid(a0_ref[...] + jnp.dot(ad, wa_ref[...], preferred_element_type=F32))
    a0_o[...] = a[:, :w].astype(a0_o.dtype)
    a1_o[...] = a[:, w:].astype(a1_o.dtype)
    g_o[...] = jnp.dot(gd, gup_ref[...], preferred_element_type=F32).astype(g_o.dtype)
    r_o[...] = shifted(0, w).astype(r_o.dtype)
    k = shifted(w, 2 * w)
    k_o[...] = k.astype(k_o.dtype)
    v_o[...] = shifted(2 * w, 3 * w).astype(v_o.dtype)
    kk = k * kkw_ref[...]
    nrm = jnp.sqrt(_head_sum(kk * kk))
    kk_o[...] = (kk / jnp.maximum(nrm, L2_EPS)).astype(kk_o.dtype)


def _rwkv_prep(p, mu_prev, mu_next, decay_w0, decay_up, iclr_a0, iclr_up, k_k, gate_up, seq):
    t = p.shape[0]
    w = RWKV_WIDTH
    tm = _tile(seq, 256)
    tps = seq // tm
    nblk8 = t // 8
    z = jnp.zeros((DECAY_LORA, w), F32)
    wdec = jnp.concatenate([jnp.concatenate([decay_up[0], z], 0), jnp.concatenate([z, decay_up[1]], 0)], 1).astype(BF16)
    wa = jnp.concatenate([jnp.concatenate([iclr_up[0], z], 0), jnp.concatenate([z, iclr_up[1]], 0)], 1).astype(BF16)
    full = lambda shape: pl.BlockSpec(shape, lambda i: (0,) * len(shape))
    blk = pl.BlockSpec((tm, w), lambda i: (i, 0))
    outs = pl.pallas_call(
        functools.partial(_rwkv_prep_kernel, tiles_per_seq=tps),
        grid=(t // tm,),
        in_specs=[pl.BlockSpec((tm, RWKV_COLS), lambda i: (i, 0)),
                  pl.BlockSpec((8, RWKV_COLS), lambda i: (jnp.maximum(i * (tm // 8) - 1, 0), 0)),
                  pl.BlockSpec((8, RWKV_COLS), lambda i: (jnp.minimum((i + 1) * (tm // 8), nblk8 - 1), 0)),
                  full((1, RWKV_COLS)), full((1, RWKV_COLS)),
                  full((2 * DECAY_LORA, 2 * w)), full((1, 2 * w)),
                  full((2 * ICLR_LORA, 2 * w)), full((1, 2 * w)),
                  full((GATE_LORA, w)), full((1, w))],
        out_specs=[blk] * 9,
        out_shape=([jax.ShapeDtypeStruct((t, w), BF16)] * 4 + [jax.ShapeDtypeStruct((t, w), F32)] * 2
                   + [jax.ShapeDtypeStruct((t, w), BF16)] * 3),
        compiler_params=_cparams(("arbitrary",)),
        name="rwkv_prep",
    )(p, p, p, mu_prev.reshape(1, -1), mu_next.reshape(1, -1), wdec, decay_w0.reshape(1, 2 * w),
      wa, iclr_a0.reshape(1, 2 * w), gate_up.astype(BF16), k_k.reshape(1, w))
    return outs


def _split2(x):
    hi = x.astype(BF16)
    return hi, (x - hi.astype(F32)).astype(BF16)


def _scan_chunks(chains):
    c, g = chains[0][0].shape
    nh = g // RWKV_HEAD
    pc = nh * c
    lp = dict(preferred_element_type=F32)
    nt = (((1,), (1,)), ((), ()))
    tn = (((0,), (0,)), ((), ()))
    fwds = [ch[7] for ch in chains]
    n = len(chains)
    each = lambda f, *ls: [f(*xs) for xs in zip(*ls)]

    ti = lax.broadcasted_iota(jnp.int32, (c, c), 0)
    si = lax.broadcasted_iota(jnp.int32, (c, c), 1)
    tp = lax.broadcasted_iota(jnp.int32, (c, pc), 0)
    sp = lax.broadcasted_iota(jnp.int32, (c, pc), 1) & (c - 1)
    tri = {f: jnp.where((si <= ti) if f else (si >= ti), 1.0, 0.0).astype(BF16) for f in set(fwds)}
    strict = {f: (sp < tp) if f else (sp > tp) for f in set(fwds)}
    incl = {f: (sp <= tp) if f else (sp >= tp) for f in set(fwds)}
    eye = jnp.where(sp == tp, 1.0, 0.0).astype(F32)
    lane_head = lax.broadcasted_iota(jnp.int32, (c, g), 1) >> 6
    rh = lax.broadcasted_iota(jnp.int32, (g, g), 0) >> 6
    chh = lax.broadcasted_iota(jnp.int32, (g, g), 1) >> 6

    def bd(x):
        z = jnp.zeros_like(x)
        return jnp.concatenate([jnp.where(lane_head == h, x, z) for h in range(nh)], axis=0)

    def mm(x, y):
        return jnp.dot(x.astype(BF16), bd(y.astype(BF16)), **lp)

    r, k, v, kk, lw, a = [[ch[i] for ch in chains] for i in range(6)]
    st_refs = [ch[6] for ch in chains]

    def cum(f, lw_):
        hi, lo = _split2(lw_)
        return jnp.dot(tri[f], hi, **lp) + jnp.dot(tri[f], lo, **lp)

    lc = each(cum, fwds, lw)
    ltot = each(lambda f, x: x[c - 1:c, :] if f else x[0:1, :], fwds, lc)
    e_inc = each(jnp.exp, lc)
    e_inv = each(lambda x: jnp.exp(-x), lc)
    e_rem = each(lambda lt, x: jnp.exp(lt - x), ltot, lc)
    kd = each(lambda k_, a_, ka_: k_ * (1.0 + (a_ - 1.0) * ka_), k, a, [ch[8] for ch in chains])
    b = each(lambda kk_, a_: kk_ * a_, kk, a)
    a_t = each(lambda kk_, lc_, lw_: (-kk_ * jnp.exp(lc_ - lw_)).astype(BF16), kk, lc, lw)
    r_t = each(lambda r_, e: (r_ * e).astype(BF16), r, e_inc)
    b_t = each(lambda b_, e: (b_ * e).astype(BF16), b, e_inv)
    k_t = each(lambda kd_, e: (kd_ * e).astype(BF16), kd, e_inv)
    b_h = each(lambda b_, e: (b_ * e).astype(BF16), b, e_rem)
    k_h = each(lambda kd_, e: (kd_ * e).astype(BF16), kd, e_rem)
    vb = each(lambda v_: v_.astype(BF16), v)

    ar = each(lambda x, y: jnp.concatenate([x, y], axis=0), a_t, r_t)
    m = each(lambda ar_, b_, k_: lax.dot_general(ar_, jnp.concatenate([bd(b_), bd(k_)], axis=0), nt, **lp),
             ar, b_t, k_t)
    n1 = each(lambda f, m_: jnp.where(strict[f], m_[:c, :pc], 0.0), fwds, m)
    m_ak = each(lambda f, m_: jnp.where(strict[f], m_[:c, pc:], 0.0).astype(BF16), fwds, m)
    m_r = each(lambda f, m_: jnp.concatenate([jnp.where(incl[f], m_[c:, :pc], 0.0),
                                              jnp.where(incl[f], m_[c:, pc:], 0.0)], axis=1).astype(BF16), fwds, m)

    pinv = each(lambda x: eye + x, n1)
    nk = each(mm, n1, n1)
    for _ in range(int(math.log2(c)) - 2):
        rr = each(lambda p_, nk_: mm(jnp.concatenate([p_, nk_], axis=0), nk_), pinv, nk)
        pinv = each(lambda p_, rr_: p_ + rr_[:c], pinv, rr)
        nk = [rr_[c:] for rr_ in rr]
    pinv = each(lambda p_, nk_: p_ + mm(p_, nk_), pinv, nk)

    st = [ref[...] for ref in st_refs]
    xs = each(lambda ar_, st_: lax.dot_general(ar_, st_.astype(BF16), nt, **lp), ar, st)
    x1 = each(lambda xs_, mak, vb_: xs_[:c] + jnp.dot(mak, bd(vb_), **lp), xs, m_ak, vb)
    ub = each(lambda p_, x_: mm(p_, x_).astype(BF16), pinv, x1)
    y = each(lambda xs_, mr, ub_, vb_: xs_[c:] + jnp.dot(mr, jnp.concatenate([bd(ub_), bd(vb_)], axis=0), **lp),
             xs, m_r, ub, vb)
    ds = each(lambda ub_, vb_, bh, kh: lax.dot_general(jnp.concatenate([ub_, vb_], axis=0),
                                                       jnp.concatenate([bh, kh], axis=0), tn, **lp),
              ub, vb, b_h, k_h)
    for i in range(n):
        st_refs[i][...] = st[i] * jnp.exp(ltot[i]) + jnp.where(rh == chh, ds[i], 0.0)
    return y


def _scan_kernel(rf, kf, vf, kkf, lwf, af, rb, kb, vb, kkb, lwb, ab, ka_ref, yf_o, yb_o, stf, stb):
    @pl.when(pl.program_id(1) == 0)
    def _():
        stf[...] = jnp.zeros_like(stf)
        stb[...] = jnp.zeros_like(stb)

    nb = rf.shape[0]
    g = SCAN_GROUP
    halves = rf.shape[2] // g
    chains, outs = [], []
    for hh in range(halves):
        cs = slice(hh * g, (hh + 1) * g)
        ld = lambda ref, b: ref[b, :, cs].astype(F32)
        ka = ka_ref[:, cs]
        for b in range(nb):
            chains.append((ld(rf, b), ld(kf, b), ld(vf, b), ld(kkf, b), ld(lwf, b), ld(af, b),
                           stf.at[hh * nb + b], True, ka))
            outs.append((yf_o, b, cs))
            chains.append((ld(rb, b), ld(kb, b), ld(vb, b), ld(kkb, b), ld(lwb, b), ld(ab, b),
                           stb.at[hh * nb + b], False, ka))
            outs.append((yb_o, b, cs))
    ys = _scan_chunks(chains)
    for (o_ref, b, cs), y in zip(outs, ys):
        o_ref[b, :, cs] = y


def _rwkv_scan(r, k, v, kk, lw0, lw1, a0, a1, k_a, batch, seq):
    t, w = r.shape
    c = _tile(seq, SCAN_CHUNK)
    g = SCAN_GROUP
    gw = 2 * g
    nc = seq // c
    fwd = pl.BlockSpec((batch, c, gw), lambda gi, ci: (0, ci, gi))
    bwd = pl.BlockSpec((batch, c, gw), lambda gi, ci: (0, nc - 1 - ci, gi))
    b3 = lambda z: z.reshape(batch, seq, w)
    nst = batch * (gw // g)
    yf, yb = pl.pallas_call(
        _scan_kernel,
        grid=(w // gw, nc),
        in_specs=[fwd] * 6 + [bwd] * 6 + [pl.BlockSpec((1, gw), lambda gi, ci: (0, gi))],
        out_specs=[fwd, bwd],
        out_shape=[jax.ShapeDtypeStruct((batch, seq, w), F32)] * 2,
        scratch_shapes=[pltpu.VMEM((nst, g, g), F32), pltpu.VMEM((nst, g, g), F32)],
        compiler_params=_cparams(("arbitrary", "arbitrary")),
        name="rwkv_scan",
    )(b3(r), b3(k), b3(v), b3(kk), b3(lw0), b3(a0), b3(r), b3(k), b3(v), b3(kk), b3(lw1), b3(a1), k_a.reshape(1, w))
    return yf.reshape(t, w), yb.reshape(t, w)


def _rwkv_post_kernel(yf, yb, r, k, v, a0, a1, g, ka, rk, gg, gb, o_ref):
    inv = 1.0 / RWKV_HEAD
    y = yf[...] + yb[...]
    mean = _head_sum(y) * inv
    yc = y - mean
    var = _head_sum(yc * yc) * inv
    yn = yc * lax.rsqrt(var + GN_EPS) * gg[...] + gb[...]
    f = lambda ref: ref[...].astype(F32)
    ksum = f(k) * (2.0 + (f(a0) + f(a1) - 2.0) * ka[...])
    bonus = _head_sum(f(r) * ksum * rk[...]) * f(v)
    o_ref[...] = ((yn + bonus) * f(g)).astype(o_ref.dtype)


def _rwkv_post(yf, yb, r, k, v, a0, a1, g, k_a, r_k, gn_gain, gn_bias):
    t, w = yf.shape
    tm = _tile(t, 256)
    blk = pl.BlockSpec((tm, w), lambda i: (i, 0))
    row = pl.BlockSpec((1, w), lambda i: (0, 0))
    return pl.pallas_call(
        _rwkv_post_kernel,
        grid=(t // tm,),
        in_specs=[blk] * 8 + [row] * 4,
        out_specs=blk,
        out_shape=jax.ShapeDtypeStruct((t, w), BF16),
        compiler_params=_cparams(("arbitrary",)),
        name="rwkv_post",
    )(yf, yb, r, k, v, a0, a1, g, k_a.reshape(1, w), r_k.reshape(1, w), gn_gain.reshape(1, w), gn_bias.reshape(1, w))


def _route_tile(sc, sel):
    rows = lambda m: [m[e:e + 1, :] for e in range(N_EXPERTS)]
    sel_r = rows(sel)
    sc_r = rows(sc)
    npg = EXPERTS_PER_GROUP

    def top2_sum(a, b, c, d):
        return jnp.maximum(jnp.maximum(jnp.maximum(a + b, a + c), jnp.maximum(a + d, b + c)),
                           jnp.maximum(b + d, c + d))

    best = top2_sum(*sel_r[0:npg])
    grp = jnp.zeros_like(best, dtype=jnp.int32)
    for gi in range(1, N_GROUPS):
        gs = top2_sum(*sel_r[gi * npg:(gi + 1) * npg])
        upd = gs > best
        grp = jnp.where(upd, gi, grp)
        best = jnp.where(upd, gs, best)

    def in_group(r, j):
        out = r[(N_GROUPS - 1) * npg + j]
        for gi in range(N_GROUPS - 2, -1, -1):
            out = jnp.where(grp == gi, r[gi * npg + j], out)
        return out

    v = [in_group(sel_r, j) for j in range(npg)]
    s = [in_group(sc_r, j) for j in range(npg)]
    b1, i1, s1 = v[0], jnp.zeros_like(grp), s[0]
    for j in range(1, npg):
        upd = v[j] > b1
        b1 = jnp.where(upd, v[j], b1)
        i1 = jnp.where(upd, j, i1)
        s1 = jnp.where(upd, s[j], s1)
    neg = jnp.full_like(b1, -jnp.inf)
    b2, i2, s2 = neg, jnp.zeros_like(grp), s[0]
    for j in range(npg):
        upd = jnp.where(i1 == j, neg, v[j]) > b2
        b2 = jnp.where(upd, v[j], b2)
        i2 = jnp.where(upd, j, i2)
        s2 = jnp.where(upd, s[j], s2)
    den = s1 + s2
    return grp * npg + i1, grp * npg + i2, s1 / den, s2 / den


def _outproj_kernel(yd_ref, yr_ref, wo_ref, x_ref, g1_ref, gain_ref, sc_ref, sh_ref, wrt_ref, bias_ref,
                    x_o, hp_o, eid_o, rank_o, gate_o, cnt_o, run_ref):
    @pl.when(pl.program_id(0) == 0)
    def _():
        run_ref[...] = jnp.zeros_like(run_ref)

    tm = x_ref.shape[0]
    half = DIFF_WIDTH
    mix = (jnp.dot(yd_ref[...], wo_ref[:half, :], preferred_element_type=F32)
           + jnp.dot(yr_ref[...], wo_ref[half:, :], preferred_element_type=F32))
    xn = x_ref[...] + g1_ref[0] * mix
    x_o[...] = xn
    hb = _norm_mod(xn, gain_ref[...], sc_ref[0], sh_ref[0]).astype(BF16)
    u = pltpu.bitcast(hb.astype(F32), U32)
    hd = D_MODEL // 2
    hp_o[...] = (u[:, :hd] & jnp.uint32(0xFFFF0000)) | (u[:, hd:] >> 16)

    logits = lax.dot_general(wrt_ref[...], hb, (((1,), (1,)), ((), ())), preferred_element_type=F32)
    sc = jax.nn.sigmoid(logits)
    e1, e2, g1, g2 = _route_tile(sc, sc + bias_ref[...])
    eid_o[0:1, :] = e1
    eid_o[1:2, :] = e2
    eio = lax.broadcasted_iota(jnp.int32, (N_EXPERTS, tm), 0)
    oh1 = eio == e1
    oh2 = eio == e2
    oh = jnp.where(oh1, 1.0, 0.0) + jnp.where(oh2, 1.0, 0.0)
    before = (lax.broadcasted_iota(jnp.int32, (tm, tm), 0) < lax.broadcasted_iota(jnp.int32, (tm, tm), 1))
    base = (jnp.dot(oh.astype(BF16), jnp.where(before, 1.0, 0.0).astype(BF16), preferred_element_type=F32)
            + run_ref[:, 0:1])
    rank_o[0:1, :] = jnp.sum(jnp.where(oh1, base, 0.0), axis=0, keepdims=True).astype(jnp.int32)
    rank_o[1:2, :] = jnp.sum(jnp.where(oh2, base, 0.0), axis=0, keepdims=True).astype(jnp.int32)
    run = run_ref[...] + jnp.sum(oh, axis=1, keepdims=True)
    run_ref[...] = run
    cnt_o[...] = run
    ri = lax.broadcasted_iota(jnp.int32, (LANES, tm), 0)
    gate_o[...] = jnp.transpose(jnp.where(ri == 0, g1, jnp.where(ri == 1, g2, 0.0)))


def _out_proj(yd, yr, w_out, x2, g1, gain, sc, sh, w_router, router_bias, seq):
    t, d = x2.shape
    tm = _tile(seq, 256)
    tps = seq // tm
    mod = pl.BlockSpec((1, 1, d), lambda i: (i // tps, 0, 0))
    ne = N_EXPERTS
    return pl.pallas_call(
        _outproj_kernel,
        grid=(t // tm,),
        in_specs=[pl.BlockSpec((tm, DIFF_WIDTH), lambda i: (i, 0)),
                  pl.BlockSpec((tm, RWKV_WIDTH), lambda i: (i, 0)),
                  pl.BlockSpec((d, d), lambda i: (0, 0)),
                  pl.BlockSpec((tm, d), lambda i: (i, 0)),
                  mod,
                  pl.BlockSpec((1, d), lambda i: (0, 0)),
                  mod, mod,
                  pl.BlockSpec((ne, d), lambda i: (0, 0)),
                  pl.BlockSpec((ne, 1), lambda i: (0, 0))],
        out_specs=[pl.BlockSpec((tm, d), lambda i: (i, 0)),
                   pl.BlockSpec((tm, d // 2), lambda i: (i, 0)),
                   pl.BlockSpec((TOP_K, tm), lambda i: (0, i)),
                   pl.BlockSpec((TOP_K, tm), lambda i: (0, i)),
                   pl.BlockSpec((tm, LANES), lambda i: (i, 0)),
                   pl.BlockSpec((ne, LANES), lambda i: (0, 0))],
        out_shape=[jax.ShapeDtypeStruct((t, d), F32),
                   jax.ShapeDtypeStruct((t, d // 2), U32),
                   jax.ShapeDtypeStruct((TOP_K, t), jnp.int32),
                   jax.ShapeDtypeStruct((TOP_K, t), jnp.int32),
                   jax.ShapeDtypeStruct((t, LANES), F32),
                   jax.ShapeDtypeStruct((ne, LANES), F32)],
        scratch_shapes=[pltpu.VMEM((ne, LANES), F32)],
        compiler_params=_cparams(("arbitrary",)),
        name="out_proj",
    )(yd, yr, w_out, x2, g1, gain.reshape(1, d), sc, sh, w_router.T.astype(BF16),
      router_bias.astype(F32).reshape(ne, 1))


def _expert_layout(counts, n_blocks):
    counts = counts.astype(jnp.int32)
    padded = ((counts + MOE_BM - 1) // MOE_BM) * MOE_BM
    padded_ends = jnp.cumsum(padded)
    offsets = (padded_ends - padded).astype(jnp.int32)
    block_start = jnp.arange(n_blocks, dtype=jnp.int32) * MOE_BM
    block_e = jnp.minimum(jnp.sum((block_start[:, None] >= padded_ends[None, :]).astype(jnp.int32), axis=1),
                          N_EXPERTS - 1).astype(jnp.int32)
    n_used = (padded_ends[-1] // MOE_BM).astype(jnp.int32).reshape(1)
    return offsets, block_e, n_used


def _dispatch_kernel(dest_ref, h_ref, xin_ref, xr_ref, hbuf, sem_in, sem_out, *, tm):
    del xin_ref
    i = pl.program_id(0)
    n = pl.num_programs(0)
    t = dest_ref.shape[0] // TOP_K
    slot = i % 2

    def load(tile, s):
        return pltpu.make_async_copy(h_ref.at[pl.ds(tile * tm, tm), :], hbuf.at[s], sem_in.at[s])

    def wait_out(s):
        for kx in range(TOP_K):
            pltpu.make_async_copy(hbuf.at[s], xr_ref.at[pl.ds(0, tm), :], sem_out.at[s]).wait()

    @pl.when(i == 0)
    def _():
        load(0, 0).start()

    load(i, slot).wait()

    def issue(r, carry):
        for kx in range(TOP_K):
            d = dest_ref[kx * t + i * tm + r]
            pltpu.make_async_copy(hbuf.at[slot, pl.ds(r, 1), :], xr_ref.at[pl.ds(d, 1), :], sem_out.at[slot]).start()
        return carry

    lax.fori_loop(0, tm, issue, 0, unroll=8)

    @pl.when(i > 0)
    def _():
        wait_out(1 - slot)

    @pl.when(i + 1 < n)
    def _():
        load(i + 1, 1 - slot).start()

    @pl.when(i == n - 1)
    def _():
        wait_out(slot)


def _dispatch(hp, dest, n_rows):
    t, hd = hp.shape
    tm = _tile(t, 256)
    zeros = jnp.zeros((n_rows, hd), U32)
    return pl.pallas_call(
        functools.partial(_dispatch_kernel, tm=tm),
        grid_spec=pltpu.PrefetchScalarGridSpec(
            num_scalar_prefetch=1,
            grid=(t // tm,),
            in_specs=[pl.BlockSpec(memory_space=pl.ANY),
                      pl.BlockSpec(memory_space=pl.ANY)],
            out_specs=pl.BlockSpec(memory_space=pl.ANY),
            scratch_shapes=[pltpu.VMEM((2, tm, hd), U32), pltpu.SemaphoreType.DMA((2,)),
                            pltpu.SemaphoreType.DMA((2,))]),
        out_shape=jax.ShapeDtypeStruct((n_rows, hd), U32),
        input_output_aliases={2: 0},
        compiler_params=_cparams(("arbitrary",)),
        name="moe_dispatch",
    )(dest, hp, zeros)


FFN_STAGE_CHUNKS = 8
FFN_STAGE_SLOTS = 4


def _ffn_kernel(be_ref, nu_ref, x_ref, wg_hbm, wu_hbm, wd_hbm, o_ref, wg_s, wu_s, wd_s, st_a, st_b, sem_a, sem_b,
                *, layer):
    i = pl.program_id(0)
    e = be_ref[i]
    active = i < nu_ref[0]
    fresh = jnp.logical_and(active, jnp.logical_or(i == 0, e != be_ref[jnp.maximum(i - 1, 0)]))

    @pl.when(fresh)
    def _():
        jobs = []
        for src, dst, stage, sem in ((wg_hbm, wg_s, st_a, sem_a), (wu_hbm, wu_s, st_a, sem_a),
                                     (wd_hbm, wd_s, st_b, sem_b)):
            rows = dst.shape[0] // FFN_STAGE_CHUNKS
            for c in range(FFN_STAGE_CHUNKS):
                jobs.append((src, dst, stage, sem, rows, c, c % FFN_STAGE_SLOTS))

        def copy(j):
            src, _, stage, sem, rows, c, slot = jobs[j]
            return pltpu.make_async_copy(src.at[layer, e, pl.ds(c * rows, rows), :], stage.at[slot], sem.at[slot])

        ahead = FFN_STAGE_SLOTS - 1
        for j in range(ahead):
            copy(j).start()
        for j in range(len(jobs)):
            if j + ahead < len(jobs):
                copy(j + ahead).start()
            copy(j).wait()
            _, dst, stage, _, rows, c, slot = jobs[j]
            dst[c * rows:(c + 1) * rows, :] = stage[slot].astype(BF16)

    @pl.when(active)
    def _():
        xp = x_ref[...]
        xa = pltpu.bitcast(xp & jnp.uint32(0xFFFF0000), F32).astype(BF16)
        xb = pltpu.bitcast(xp << 16, F32).astype(BF16)
        x = jnp.concatenate([xa, xb], axis=1)
        hg = jnp.dot(x, wg_s[...], preferred_element_type=F32)
        hu = jnp.dot(x, wu_s[...], preferred_element_type=F32)
        h = (hg * jax.nn.sigmoid(hg) * hu).astype(BF16)
        o_ref[...] = jnp.dot(h, wd_s[...], preferred_element_type=F32)

    @pl.when(jnp.logical_not(active))
    def _():
        o_ref[...] = jnp.zeros_like(o_ref)


def _expert_ffn(xr, block_e, n_used, wg, wu, wd, layer):
    p, hd = xr.shape
    d = 2 * hd
    nb = p // MOE_BM
    sc = FFN_STAGE_CHUNKS
    hbm = pl.BlockSpec(memory_space=pl.ANY)
    return pl.pallas_call(
        functools.partial(_ffn_kernel, layer=layer),
        grid_spec=pltpu.PrefetchScalarGridSpec(
            num_scalar_prefetch=2,
            grid=(nb,),
            in_specs=[pl.BlockSpec((MOE_BM, hd), lambda i, be, nu: (i, 0)), hbm, hbm, hbm],
            out_specs=pl.BlockSpec((MOE_BM, d), lambda i, be, nu: (i, 0)),
            scratch_shapes=[pltpu.VMEM((d, D_FF), BF16), pltpu.VMEM((d, D_FF), BF16), pltpu.VMEM((D_FF, d), BF16),
                            pltpu.VMEM((FFN_STAGE_SLOTS, d // sc, D_FF), F32),
                            pltpu.VMEM((FFN_STAGE_SLOTS, D_FF // sc, d), F32),
                            pltpu.SemaphoreType.DMA((FFN_STAGE_SLOTS,)), pltpu.SemaphoreType.DMA((FFN_STAGE_SLOTS,))]),
        out_shape=jax.ShapeDtypeStruct((p, d), F32),
        compiler_params=_cparams(("arbitrary",)),
        name="moe_ffn",
    )(block_e, n_used, xr, wg, wu, wd)


def _combine_kernel(dest_ref, yr_ref, gate_ref, x_ref, g2_ref, fg_ref, o_ref, buf, sem, *, final):
    i = pl.program_id(0)
    n = pl.num_programs(0)
    tm = x_ref.shape[0]
    t = dest_ref.shape[0] // TOP_K

    def issue_tile(tile, slot):
        def issue(r, carry):
            for kx in range(TOP_K):
                d = dest_ref[kx * t + tile * tm + r]
                pltpu.make_async_copy(yr_ref.at[pl.ds(d, 1), :], buf.at[slot, kx, pl.ds(r, 1), :],
                                      sem.at[slot]).start()
            return carry

        lax.fori_loop(0, tm, issue, 0, unroll=8)

    @pl.when(i == 0)
    def _():
        issue_tile(0, 0)

    @pl.when(i + 1 < n)
    def _():
        issue_tile(i + 1, (i + 1) % 2)

    slot = i % 2
    for kx in range(TOP_K):
        pltpu.make_async_copy(yr_ref.at[pl.ds(0, tm), :], buf.at[slot, kx], sem.at[slot]).wait()

    gate = gate_ref[...]
    y = gate[:, 0:1] * buf[slot, 0] + gate[:, 1:2] * buf[slot, 1]
    xn = x_ref[...] + g2_ref[0] * y
    if final:
        ms = jnp.mean(xn * xn, axis=-1, keepdims=True)
        xn = xn * lax.rsqrt(ms + RMS_EPS) * fg_ref[...]
    o_ref[...] = xn


def _combine(yr, dest, gate, x2, g2, final_gain, seq, final):
    t, d = x2.shape
    tm = _tile(seq, 256)
    tps = seq // tm
    return pl.pallas_call(
        functools.partial(_combine_kernel, final=final),
        grid_spec=pltpu.PrefetchScalarGridSpec(
            num_scalar_prefetch=1,
            grid=(t // tm,),
            in_specs=[pl.BlockSpec(memory_space=pl.ANY),
                      pl.BlockSpec((tm, LANES), lambda i, *_: (i, 0)),
                      pl.BlockSpec((tm, d), lambda i, *_: (i, 0)),
                      pl.BlockSpec((1, 1, d), lambda i, *_: (i // tps, 0, 0)),
                      pl.BlockSpec((1, d), lambda i, *_: (0, 0))],
            out_specs=pl.BlockSpec((tm, d), lambda i, *_: (i, 0)),
            scratch_shapes=[pltpu.VMEM((2, TOP_K, tm, d), F32), pltpu.SemaphoreType.DMA((2,))]),
        out_shape=jax.ShapeDtypeStruct((t, d), F32),
        compiler_params=_cparams(("arbitrary",)),
        name="moe_combine",
    )(dest, yr, gate, x2, g2, final_gain.reshape(1, d))


def kernel(x, c, positions, w_ada, b_ada, norm_gain, w_in, w_out, mu_prev, mu_next, decay_w0, decay_up, iclr_a0, iclr_up, k_k, k_a, r_k, gate_up, gn_gain, gn_bias, lam_q1, lam_k1, lam_q2, lam_k2, diff_sub_gain, w_router, router_bias, w_gate, w_up, w_down, final_gain):
    batch, seq, d = x.shape
    depth = w_ada.shape[0]
    t = batch * seq
    w = RWKV_WIDTH
    x2 = x.reshape(t, d)
    mod = _ada_mod(c, w_ada, b_ada)
    rope_tabs = _rope_tables(positions)
    rope_tabs_t = _rope_tables_t(positions)
    n_rows = t * TOP_K + N_EXPERTS * MOE_BM
    for l in range(depth):
        sh1, sc1, g1, sh2, sc2, g2 = [mod[l, :, i * d:(i + 1) * d].reshape(batch, 1, d) for i in range(N_MOD)]
        w_l = w_in[l].astype(BF16)
        dw = DIFF_WIDTH
        qt, kq, vt = _project_qkv(x2, norm_gain[l, 0], sc1, sh1, w_l[:, :dw].T, w_l[:, dw:2 * dw],
                                  w_l[:, 2 * dw:3 * dw].T, seq, rope_tabs, rope_tabs_t)
        p = _project(x2, norm_gain[l, 0], sc1, sh1, w_l[:, DIFF_COLS:], seq, F32, 1152)
        lam_init = 0.8 - 0.6 * math.exp(-0.3 * l)
        lam = (jnp.exp(jnp.sum(lam_q1[l].astype(F32) * lam_k1[l].astype(F32)))
               - jnp.exp(jnp.sum(lam_q2[l].astype(F32) * lam_k2[l].astype(F32))) + lam_init)
        y_diff = _diff_attention(qt, kq, vt, lam, diff_sub_gain[l], lam_init, batch, seq)
        r, k, v, kk, lw0, lw1, a0, a1, g = _rwkv_prep(p, mu_prev[l], mu_next[l], decay_w0[l], decay_up[l],
                                                      iclr_a0[l], iclr_up[l], k_k[l], gate_up[l], seq)
        yf, yb = _rwkv_scan(r, k, v, kk, lw0, lw1, a0, a1, k_a[l], batch, seq)
        y_rwkv = _rwkv_post(yf, yb, r, k, v, a0, a1, g, k_a[l], r_k[l], gn_gain[l], gn_bias[l])
        x2, hp, eid, rank, gate, cnt = _out_proj(y_diff, y_rwkv, w_out[l].astype(BF16), x2, g1, norm_gain[l, 1],
                                                 sc2, sh2, w_router, router_bias, seq)
        offsets, block_e, n_used = _expert_layout(cnt[:, 0], n_rows // MOE_BM)
        onehot = eid[:, :, None] == jnp.arange(N_EXPERTS, dtype=jnp.int32)
        dest = (rank + jnp.sum(jnp.where(onehot, offsets, 0), axis=-1)).reshape(TOP_K * t)
        xr = _dispatch(hp, dest, n_rows)
        yr = _expert_ffn(xr, block_e, n_used, w_gate, w_up, w_down, l)
        x2 = _combine(yr, dest, gate, x2, g2, final_gain, seq, final=(l == depth - 1))
    return x2.reshape(batch, seq, d)
```
